```python
import math
import jax, jax.numpy as jnp
from jax import lax
import numpy as np

D_MODEL = 1024
BATCH = 4
SEQ = 8192
DEPTH = 1

D_MIX = D_MODEL
DA_HEADS = 4
DA_HEAD_DIM = 64
DA_V_DIM = 2 * DA_HEAD_DIM
DA_WIDTH = DA_HEADS * DA_V_DIM
RW_HEAD = 64
RW_WIDTH = D_MIX - DA_WIDTH
RW_HEADS = RW_WIDTH // RW_HEAD
DECAY_LORA = 64
AAA_LORA = 64
GATE_LORA = 128
DA_COLS = 3 * DA_WIDTH
RW_COLS = 3 * RW_WIDTH + DECAY_LORA + AAA_LORA + GATE_LORA
IN_COLS = DA_COLS + RW_COLS
ROPE_THETA = 10000.0
Q_BLOCK = 128
N_EXPERTS = 32
TOP_K = 4
D_FF = D_MODEL
SWIGLU_ALPHA = 1.702
SWIGLU_LIMIT = 7.0
MOE_BLOCK = 128
NORM_EPS = 1e-5
GN_EPS = 64e-5

kernel_name = "hybrid_diffattn_rwkv7_moe"

F32 = jnp.float32


def rmsnorm(x, w):
    x32 = x.astype(F32)
    y = x32 * lax.rsqrt(jnp.mean(x32 * x32, axis=-1, keepdims=True) + NORM_EPS)
    return (y * w.astype(F32)).astype(x.dtype)


def rope_tables(positions):
    inv = ROPE_THETA ** (-jnp.arange(0, DA_HEAD_DIM, 2, dtype=F32) / DA_HEAD_DIM)
    ang = positions.astype(F32)[..., None] * inv
    return jnp.cos(ang), jnp.sin(ang)


def apply_rope(x, cos, sin):
    c = cos[:, :, None, None, :]
    s = sin[:, :, None, None, :]
    x1, x2 = jnp.split(x.astype(F32), 2, axis=-1)
    return jnp.concatenate([x1 * c - x2 * s, x2 * c + x1 * s], axis=-1).astype(x.dtype)


def diff_attention(q, k, v, lam, lam_init, subln_w):
    B, S = q.shape[0], q.shape[1]
    nb = S // Q_BLOCK
    scale = DA_HEAD_DIM ** -0.5
    kt = jnp.transpose(k, (0, 2, 3, 1, 4))
    vt = jnp.transpose(v, (0, 2, 1, 3))
    qb = q.reshape(B, nb, Q_BLOCK, DA_HEADS, 2, DA_HEAD_DIM).transpose(1, 0, 3, 4, 2, 5)
    key_pos = jnp.arange(S)

    def block(args):
        qi, i = args
        s = jnp.einsum('bhcqd,bhckd->bhcqk', qi, kt, preferred_element_type=F32) * scale
        qpos = i * Q_BLOCK + jnp.arange(Q_BLOCK)
        s = jnp.where(key_pos[None, :] <= qpos[:, None], s, -jnp.inf)
        p = jax.nn.softmax(s, axis=-1)
        a = p[:, :, 0] - lam * p[:, :, 1]
        return jnp.einsum('bhqk,bhkd->bhqd', a.astype(vt.dtype), vt)

    o = lax.map(block, (qb, jnp.arange(nb)))
    o = o.transpose(1, 0, 3, 2, 4).reshape(B, S, DA_HEADS, DA_V_DIM)
    o = rmsnorm(o, subln_w) * (1.0 - lam_init)
    return o.reshape(B, S, DA_WIDTH)


def rwkv7_scan(r, w, k, v, kk, a):
    B, S_len, H, N = r.shape

    def step(st, inp):
        r_t, w_t, k_t, v_t, kk_t, a_t = inp
        sa = jnp.einsum('bhvk,bhk->bhv', st, kk_t)
        st = (st * w_t[:, :, None, :]
              - sa[..., None] * (kk_t * a_t)[:, :, None, :]
              + v_t[..., None] * k_t[:, :, None, :])
        y = jnp.einsum('bhvk,bhk->bhv', st, r_t)
        return st, y

    xs = tuple(jnp.moveaxis(t.astype(F32), 1, 0) for t in (r, w, k, v, kk, a))
    st0 = jnp.zeros((B, H, N, N), F32)
    _, y = lax.scan(step, st0, xs)
    return jnp.moveaxis(y, 0, 1)


def rwkv7_group(p, mu, w0, w2, a0, a2, g2, k_k, k_a, r_k, lnx_w, lnx_b):
    B, S = p.shape[0], p.shape[1]
    prev = jnp.pad(p, ((0, 0), (1, 0), (0, 0)))[:, :-1]
    p = p + mu * (prev - p)
    c0, c1, c2 = RW_WIDTH, 2 * RW_WIDTH, 3 * RW_WIDTH
    c3, c4 = c2 + DECAY_LORA, c2 + DECAY_LORA + AAA_LORA
    r, k, v, wd, ad, gd = jnp.split(p, [c0, c1, c2, c3, c4], axis=-1)
    wlog = -jax.nn.softplus(-(w0 + jnp.tanh(wd) @ w2).astype(F32)) - 0.5
    decay = jnp.exp(-jnp.exp(wlog))
    a = jax.nn.sigmoid((a0 + ad @ a2).astype(F32))
    g = jax.nn.sigmoid(gd) @ g2
    heads = lambda t: t.reshape(B, S, RW_HEADS, RW_HEAD)
    kk = heads((k * k_k).astype(F32))
    kk = kk / jnp.maximum(jnp.sqrt(jnp.sum(kk * kk, axis=-1, keepdims=True)), 1e-12)
    k = k.astype(F32) * (1.0 + (a - 1.0) * k_a.astype(F32))
    rh, kh, vh, ah = heads(r.astype(F32)), heads(k), heads(v.astype(F32)), heads(a)
    y = rwkv7_scan(rh, heads(decay), kh, vh, kk, ah)
    mean = jnp.mean(y, axis=-1, keepdims=True)
    var = jnp.mean(jnp.square(y - mean), axis=-1, keepdims=True)
    yn = ((y - mean) * lax.rsqrt(var + GN_EPS)).reshape(B, S, RW_WIDTH)
    yn = yn * lnx_w.astype(F32) + lnx_b.astype(F32)
    bonus = jnp.sum(rh * kh * r_k.astype(F32), axis=-1, keepdims=True) * vh
    out = (yn + bonus.reshape(B, S, RW_WIDTH)) * g.astype(F32)
    return out.astype(p.dtype)


def moe_ffn(h, router_w, router_b, mlp1_w, mlp1_b, mlp2_w, mlp2_b):
    B, S, D = h.shape
    T = B * S
    xf = h.reshape(T, D)
    logits = (xf @ router_w + router_b).astype(F32)
    top_val, top_idx = lax.top_k(logits, TOP_K)
    gates = jax.nn.softmax(top_val, axis=-1)
    M = T * TOP_K
    e_flat = top_idx.reshape(M)
    tok_flat = jnp.arange(M, dtype=jnp.int32) // TOP_K
    gate_flat = gates.reshape(M)
    order = jnp.argsort(e_flat)
    e_s, tok_s, gate_s = e_flat[order], tok_flat[order], gate_flat[order]
    counts = jnp.bincount(e_flat, length=N_EXPERTS)
    start = jnp.cumsum(counts) - counts
    padded = (counts + MOE_BLOCK - 1) // MOE_BLOCK * MOE_BLOCK
    pend = jnp.cumsum(padded)
    pstart = pend - padded
    dest = pstart[e_s] + (jnp.arange(M) - start[e_s])
    P = M + N_EXPERTS * MOE_BLOCK
    row_tok = jnp.full((P,), T, jnp.int32).at[dest].set(tok_s)
    row_gate = jnp.zeros((P,), F32).at[dest].set(gate_s)
    nblk = P // MOE_BLOCK
    blk_expert = jnp.minimum(jnp.searchsorted(pend, jnp.arange(nblk) * MOE_BLOCK, side='right'), N_EXPERTS - 1)
    x_pad = jnp.concatenate([xf, jnp.zeros((1, D), xf.dtype)], axis=0)

    def block(args):
        toks, e = args
        xb = x_pad[toks]
        gu = xb @ mlp1_w[e] + mlp1_b[e]
        gate, up = gu[..., ::2], gu[..., 1::2]
        gate = jnp.minimum(gate, SWIGLU_LIMIT)
        up = jnp.clip(up, -SWIGLU_LIMIT, SWIGLU_LIMIT)
        act = (up + 1.0) * gate * jax.nn.sigmoid(SWIGLU_ALPHA * gate)
        return act @ mlp2_w[e] + mlp2_b[e]

    y = lax.map(block, (row_tok.reshape(nblk, MOE_BLOCK), blk_expert)).reshape(P, D)
    y = (y.astype(F32) * row_gate[:, None]).astype(h.dtype)
    out = jnp.zeros((T + 1, D), h.dtype).at[row_tok].add(y)[:T]
    return out.reshape(B, S, D)


def setup_inputs(seed: int = 0) -> dict:
    key = jax.random.key(seed)
    ks = jax.random.split(key, 32)
    L = DEPTH
    nrm = lambda k, shape, scale: jax.random.normal(k, shape, F32) * scale
    uni = lambda k, shape, lo, hi: jax.random.uniform(k, shape, F32, lo, hi)
    return {
        'x': nrm(ks[0], (BATCH, SEQ, D_MODEL), 1.0),
        'positions': jnp.broadcast_to(jnp.arange(SEQ, dtype=jnp.int32), (BATCH, SEQ)),
        'attn_norm_w': 1.0 + nrm(ks[1], (L, D_MODEL), 0.02),
        'w_in': nrm(ks[2], (L, D_MODEL, IN_COLS), D_MODEL ** -0.5),
        'shift_mu': uni(ks[3], (L, RW_COLS), 0.0, 1.0),
        'lambda_q1': nrm(ks[4], (L, DA_HEAD_DIM), 0.1),
        'lambda_k1': nrm(ks[5], (L, DA_HEAD_DIM), 0.1),
        'lambda_q2': nrm(ks[6], (L, DA_HEAD_DIM), 0.1),
        'lambda_k2': nrm(ks[7], (L, DA_HEAD_DIM), 0.1),
        'subln_w': 1.0 + nrm(ks[8], (L, DA_V_DIM), 0.02),
        'rw_w0': uni(ks[9], (L, RW_WIDTH), -6.0, 0.0),
        'rw_w2': nrm(ks[10], (L, DECAY_LORA, RW_WIDTH), 0.1 * DECAY_LORA ** -0.5),
        'rw_a0': nrm(ks[11], (L, RW_WIDTH), 0.5),
        'rw_a2': nrm(ks[12], (L, AAA_LORA, RW_WIDTH), AAA_LORA ** -0.5),
        'rw_g2': nrm(ks[13], (L, GATE_LORA, RW_WIDTH), GATE_LORA ** -0.5),
        'rw_k_k': 0.85 + nrm(ks[14], (L, RW_WIDTH), 0.02),
        'rw_k_a': 1.0 + nrm(ks[15], (L, RW_WIDTH), 0.02),
        'rw_r_k': nrm(ks[16], (L, RW_HEADS, RW_HEAD), 0.1),
        'rw_lnx_w': 1.0 + nrm(ks[17], (L, RW_WIDTH), 0.02),
        'rw_lnx_b': nrm(ks[18], (L, RW_WIDTH), 0.01),
        'w_out': nrm(ks[19], (L, D_MIX, D_MODEL), D_MIX ** -0.5),
        'ffn_norm_w': 1.0 + nrm(ks[20], (L, D_MODEL), 0.02),
        'router_w': nrm(ks[21], (L, D_MODEL, N_EXPERTS), D_MODEL ** -0.5),
        'router_b': nrm(ks[22], (L, N_EXPERTS), 0.01),
        'mlp1_w': nrm(ks[23], (L, N_EXPERTS, D_MODEL, 2 * D_FF), D_MODEL ** -0.5),
        'mlp1_b': nrm(ks[24], (L, N_EXPERTS, 2 * D_FF), 0.01),
        'mlp2_w': nrm(ks[25], (L, N_EXPERTS, D_FF, D_MODEL), D_FF ** -0.5),
        'mlp2_b': nrm(ks[26], (L, N_EXPERTS, D_MODEL), 0.01),
        'final_norm_w': 1.0 + nrm(ks[27], (D_MODEL,), 0.02),
    }


def reference(x, positions, attn_norm_w, w_in, shift_mu, lambda_q1, lambda_k1, lambda_q2, lambda_k2,
              subln_w, rw_w0, rw_w2, rw_a0, rw_a2, rw_g2, rw_k_k, rw_k_a, rw_r_k, rw_lnx_w, rw_lnx_b,
              w_out, ffn_norm_w, router_w, router_b, mlp1_w, mlp1_b, mlp2_w, mlp2_b, final_norm_w):
    B, S = x.shape[0], x.shape[1]
    cos, sin = rope_tables(positions)
    for l in range(DEPTH):
        h = rmsnorm(x, attn_norm_w[l])
        p = h @ w_in[l]
        q, k, v, p_rw = jnp.split(p, [DA_WIDTH, 2 * DA_WIDTH, 3 * DA_WIDTH], axis=-1)
        q = apply_rope(q.reshape(B, S, DA_HEADS, 2, DA_HEAD_DIM), cos, sin)
        k = apply_rope(k.reshape(B, S, DA_HEADS, 2, DA_HEAD_DIM), cos, sin)
        v = v.reshape(B, S, DA_HEADS, DA_V_DIM)
        lam_init = 0.8 - 0.6 * math.exp(-0.3 * l)
        lam = (jnp.exp(jnp.sum(lambda_q1[l].astype(F32) * lambda_k1[l].astype(F32)))
               - jnp.exp(jnp.sum(lambda_q2[l].astype(F32) * lambda_k2[l].astype(F32))) + lam_init)
        o_da = diff_attention(q, k, v, lam, lam_init, subln_w[l])
        o_rw = rwkv7_group(p_rw, shift_mu[l], rw_w0[l], rw_w2[l], rw_a0[l], rw_a2[l], rw_g2[l],
                           rw_k_k[l], rw_k_a[l], rw_r_k[l], rw_lnx_w[l], rw_lnx_b[l])
        x = x + jnp.concatenate([o_da, o_rw.astype(o_da.dtype)], axis=-1) @ w_out[l]
        h = rmsnorm(x, ffn_norm_w[l])
        x = x + moe_ffn(h, router_w[l], router_b[l], mlp1_w[l], mlp1_b[l], mlp2_w[l], mlp2_b[l])
    return rmsnorm(x, final_norm_w)
```

```python
import functools
import math

import jax
import jax.numpy as jnp
from jax import lax
from jax.experimental import pallas as pl
from jax.experimental.pallas import tpu as pltpu

F32 = jnp.float32
BF16 = jnp.bfloat16

V7X_LANES = 128
V7X_VMEM_BYTES = 64 * 1024 * 1024
VMEM_LIMIT_BYTES = 56 * 1024 * 1024

DA_HEADS = 4
DA_HEAD_DIM = 64
DA_V_DIM = 2 * DA_HEAD_DIM
DA_WIDTH = DA_HEADS * DA_V_DIM
RW_HEAD = 64
DECAY_LORA = 64
AAA_LORA = 64
GATE_LORA = 128
ROPE_THETA = 10000.0
N_EXPERTS = 32
TOP_K = 4
SWIGLU_ALPHA = 1.702
SWIGLU_LIMIT = 7.0
NORM_EPS = 1e-5
GN_EPS = 64e-5
LAYER_INDEX = 0
LAM_INIT = 0.8 - 0.6 * math.exp(-0.3 * LAYER_INDEX)


def _cparams(*semantics):
    return pltpu.CompilerParams(dimension_semantics=semantics, vmem_limit_bytes=VMEM_LIMIT_BYTES)


def _pick_tile(n, want):
    t = min(n, want)
    while n % t:
        t //= 2
    return t


def _inproj_body(x_ref, nw_ref, w_ref, cos_ref, sin_ref, q_ref, k_ref, v_ref, prw_ref):
    x = x_ref[...]
    ms = jnp.mean(x * x, axis=-1, keepdims=True)
    h = (x * lax.rsqrt(ms + NORM_EPS)) * nw_ref[...]
    p = jnp.dot(h.astype(BF16), w_ref[...], preferred_element_type=F32)
    cos = cos_ref[...]
    sin = sin_ref[...]
    lane = lax.broadcasted_iota(jnp.int32, cos.shape, 1)
    first_half = (lane % DA_HEAD_DIM) < (DA_HEAD_DIM // 2)
    scale = DA_HEAD_DIM ** -0.5

    def rope(t):
        partner = jnp.where(first_half,
                            pltpu.roll(t, V7X_LANES - DA_HEAD_DIM // 2, 1),
                            pltpu.roll(t, DA_HEAD_DIM // 2, 1))
        return t * cos + partner * sin

    for hh in range(DA_HEADS):
        sl = slice(hh * DA_V_DIM, (hh + 1) * DA_V_DIM)
        q_ref[:, sl] = (rope(p[:, sl]) * scale).astype(q_ref.dtype)
        ksl = slice(DA_WIDTH + hh * DA_V_DIM, DA_WIDTH + (hh + 1) * DA_V_DIM)
        k_ref[:, sl] = rope(p[:, ksl]).astype(k_ref.dtype)
    v_ref[...] = p[:, 2 * DA_WIDTH:3 * DA_WIDTH].astype(v_ref.dtype)
    prw_ref[...] = p[:, 3 * DA_WIDTH:]


def _input_projection(x2, norm_w, w_in, cos_t, sin_t):
    T, D = x2.shape
    n_cols = w_in.shape[1]
    rw_cols = n_cols - 3 * DA_WIDTH
    tm = _pick_tile(T, 512)
    row = lambda i: (i, 0)
    fixed = lambda i: (0, 0)
    return pl.pallas_call(
        _inproj_body,
        grid=(T // tm,),
        in_specs=[pl.BlockSpec((tm, D), row),
                  pl.BlockSpec((1, D), fixed),
                  pl.BlockSpec((D, n_cols), fixed),
                  pl.BlockSpec((tm, DA_V_DIM), row),
                  pl.BlockSpec((tm, DA_V_DIM), row)],
        out_specs=[pl.BlockSpec((tm, DA_WIDTH), row),
                   pl.BlockSpec((tm, DA_WIDTH), row),
                   pl.BlockSpec((tm, DA_WIDTH), row),
                   pl.BlockSpec((tm, rw_cols), row)],
        out_shape=[jax.ShapeDtypeStruct((T, DA_WIDTH), BF16),
                   jax.ShapeDtypeStruct((T, DA_WIDTH), BF16),
                   jax.ShapeDtypeStruct((T, DA_WIDTH), BF16),
                   jax.ShapeDtypeStruct((T, rw_cols), F32)],
        compiler_params=_cparams("parallel"),
        name="input_projection",
    )(x2, norm_w.reshape(1, D), w_in.astype(BF16), cos_t, sin_t)


def _rope_tables(positions):
    half = DA_HEAD_DIM // 2
    inv = ROPE_THETA ** (-jnp.arange(0, DA_HEAD_DIM, 2, dtype=F32) / DA_HEAD_DIM)
    ang = positions.reshape(-1).astype(F32)[:, None] * inv
    cos, sin = jnp.cos(ang), jnp.sin(ang)
    cos_t = jnp.tile(cos, (1, DA_V_DIM // half))
    sin_t = jnp.tile(jnp.concatenate([-sin, sin], axis=-1), (1, DA_V_DIM // DA_HEAD_DIM))
    return cos_t, sin_t


def _diff_attn_body(q_ref, k_ref, v_ref, lq1_ref, lk1_ref, lq2_ref, lk2_ref, sw_ref, o_ref,
                    m_ref, l_ref, acc_ref, *, tq, tk):
    qi = pl.program_id(2)
    q = q_ref[0]
    lane = lax.broadcasted_iota(jnp.int32, q.shape, 1)
    zero = jnp.zeros_like(q)
    q_sub = (jnp.where(lane < DA_HEAD_DIM, q, zero), jnp.where(lane >= DA_HEAD_DIM, q, zero))

    m_ref[...] = jnp.full(m_ref.shape, -jnp.inf, F32)
    l_ref[...] = jnp.zeros(l_ref.shape, F32)
    acc_ref[...] = jnp.zeros(acc_ref.shape, F32)

    def step(j, masked):
        ks = pl.multiple_of(j * tk, tk)
        kb = k_ref[0, pl.ds(ks, tk), :]
        vb = v_ref[0, pl.ds(ks, tk), :]
        for c in range(2):
            s = lax.dot_general(q_sub[c], kb, (((1,), (1,)), ((), ())), preferred_element_type=F32)
            if masked:
                rows = lax.broadcasted_iota(jnp.int32, s.shape, 0)
                cols = lax.broadcasted_iota(jnp.int32, s.shape, 1)
                s = jnp.where(cols <= rows, s, -jnp.inf)
            m_prev = m_ref[c]
            m_new = jnp.maximum(m_prev, jnp.max(s, axis=-1, keepdims=True))
            alpha = jnp.exp(m_prev - m_new)
            p = jnp.exp(s - m_new)
            l_ref[c] = alpha * l_ref[c] + jnp.sum(p, axis=-1, keepdims=True)
            acc_ref[c] = alpha * acc_ref[c] + jnp.dot(p.astype(vb.dtype), vb, preferred_element_type=F32)
            m_ref[c] = m_new

    def full_step(j, carry):
        step(j, False)
        return carry

    lax.fori_loop(0, qi * (tq // tk), full_step, 0)
    for d in range(tq // tk):
        step(qi * (tq // tk) + d, True)

    lam = (jnp.exp(jnp.sum(lq1_ref[...] * lk1_ref[...], axis=-1, keepdims=True))
           - jnp.exp(jnp.sum(lq2_ref[...] * lk2_ref[...], axis=-1, keepdims=True)) + LAM_INIT)
    o = acc_ref[0] / l_ref[0] - lam * (acc_ref[1] / l_ref[1])
    o = o * lax.rsqrt(jnp.mean(o * o, axis=-1, keepdims=True) + NORM_EPS)
    o_ref[0] = (o * sw_ref[...] * (1.0 - LAM_INIT)).astype(o_ref.dtype)


def _diff_attention(q, k, v, lq1, lk1, lq2, lk2, subln_w, B, S):
    tq = tk = _pick_tile(S, 512)
    q3, k3, v3 = (t.reshape(B, S, DA_WIDTH) for t in (q, k, v))
    vec = lambda a: a.reshape(1, -1).astype(F32)
    blk = lambda b, h, i: (b, i, h)
    seq = lambda b, h, i: (b, 0, h)
    fixed = lambda b, h, i: (0, 0)
    out = pl.pallas_call(
        functools.partial(_diff_attn_body, tq=tq, tk=tk),
        grid=(B, DA_HEADS, S // tq),
        in_specs=[pl.BlockSpec((1, tq, DA_V_DIM), blk),
                  pl.BlockSpec((1, S, DA_V_DIM), seq),
                  pl.BlockSpec((1, S, DA_V_DIM), seq),
                  pl.BlockSpec((1, DA_HEAD_DIM), fixed),
                  pl.BlockSpec((1, DA_HEAD_DIM), fixed),
                  pl.BlockSpec((1, DA_HEAD_DIM), fixed),
                  pl.BlockSpec((1, DA_HEAD_DIM), fixed),
                  pl.BlockSpec((1, DA_V_DIM), fixed)],
        out_specs=pl.BlockSpec((1, tq, DA_V_DIM), blk),
        out_shape=jax.ShapeDtypeStruct((B, S, DA_WIDTH), BF16),
        scratch_shapes=[pltpu.VMEM((2, tq, 1), F32),
                        pltpu.VMEM((2, tq, 1), F32),
                        pltpu.VMEM((2, tq, DA_V_DIM), F32)],
        compiler_params=_cparams("parallel", "parallel", "arbitrary"),
        name="diff_attention",
    )(q3, k3, v3, vec(lq1), vec(lk1), vec(lq2), vec(lk2), vec(subln_w))
    return out.reshape(B * S, DA_WIDTH)


RW_CHUNK = 64
RW_PAIR = 2 * RW_HEAD
RW_INV_DOUBLINGS = 5


def _bdot(a, b):
    return jnp.dot(a.astype(BF16), b.astype(BF16), preferred_element_type=F32)


def _bdot_nt(a, b):
    return lax.dot_general(a.astype(BF16), b.astype(BF16), (((1,), (1,)), ((), ())),
                           preferred_element_type=F32)


def _bdot_tn(a, b):
    return lax.dot_general(a.astype(BF16), b.astype(BF16), (((0,), (0,)), ((), ())),
                           preferred_element_type=F32)


def _split3_dot(m, x):
    mb = m.astype(BF16)
    x0 = x.astype(BF16)
    r1 = x - x0.astype(F32)
    x1 = r1.astype(BF16)
    x2 = (r1 - x1.astype(F32)).astype(BF16)
    dot = lambda t: jnp.dot(mb, t, preferred_element_type=F32)
    return dot(x0) + dot(x1) + dot(x2)


def _rwkv_body(p_ref, prev_ref, mu_ref, w0_ref, a0_ref, kk_ref, ka_ref, rk_ref, lnw_ref, lnb_ref,
               wa_ref, g2_ref, o_ref,
               s_ref, rt_ref, kh_ref, at_ref, bh_ref, v_ref, gl_ref, y_ref, *, tb, width):
    i = pl.program_id(1)
    n_pairs = width // RW_PAIR
    C = RW_CHUNK

    @pl.when(i == 0)
    def _():
        s_ref[...] = jnp.zeros(s_ref.shape, F32)

    p = p_ref[0]
    row = lax.broadcasted_iota(jnp.int32, p.shape, 0)
    last_prev = jnp.where(i == 0, 0.0, prev_ref[0, 7:8, :])
    prev = jnp.where(row == 0, last_prev, pltpu.roll(p, 1, 0))
    ps = p + mu_ref[...] * (prev - p)
    r = ps[:, 0:width]
    k = ps[:, width:2 * width]
    v = ps[:, 2 * width:3 * width]
    c3 = 3 * width
    wa_in = ps[:, c3:c3 + DECAY_LORA + AAA_LORA]
    lane = lax.broadcasted_iota(jnp.int32, wa_in.shape, 1)
    wa_in = jnp.where(lane < DECAY_LORA, jnp.tanh(wa_in), wa_in)
    wa = _bdot(wa_in, wa_ref[...])
    gd = ps[:, c3 + DECAY_LORA + AAA_LORA:]
    g = _bdot(jax.nn.sigmoid(gd), g2_ref[...])

    wl = w0_ref[...] + wa[:, :width]
    softplus_neg = jnp.maximum(-wl, 0.0) + jnp.log1p(jnp.exp(-jnp.abs(wl)))
    logw = -jnp.exp(-softplus_neg - 0.5)
    a = jax.nn.sigmoid(a0_ref[...] + wa[:, width:])

    hr = lax.broadcasted_iota(jnp.int32, (width, width), 0) // RW_HEAD
    hc = lax.broadcasted_iota(jnp.int32, (width, width), 1) // RW_HEAD
    head_ones = jnp.where(hr == hc, 1.0, 0.0).astype(BF16)
    head_sum = lambda t: jnp.dot(t.astype(BF16), head_ones, preferred_element_type=F32)

    kk = k * kk_ref[...]
    kk = kk / jnp.maximum(jnp.sqrt(head_sum(kk * kk)), 1e-12)
    k2 = k * (1.0 + (a - 1.0) * ka_ref[...])
    bonus = head_sum(r * k2 * rk_ref[...]) * v

    tr = lax.broadcasted_iota(jnp.int32, (tb, tb), 0)
    tc = lax.broadcasted_iota(jnp.int32, (tb, tb), 1)
    chunk_tril = jnp.where((tr // C == tc // C) & (tc <= tr), 1.0, 0.0)
    cum = _split3_dot(chunk_tril, logw)
    g_inc = jnp.exp(cum)
    g_inv = jnp.exp(-cum)
    g_exc = jnp.exp(cum - logw)
    rt_ref[...] = r * g_inc
    kh_ref[...] = k2 * g_inv
    at_ref[...] = -kk * g_exc
    bh_ref[...] = kk * a * g_inv
    v_ref[...] = v
    gl_ref[...] = g_inc

    lane_p = lax.broadcasted_iota(jnp.int32, (C, RW_PAIR), 1)
    first = lane_p < RW_HEAD
    ri = lax.broadcasted_iota(jnp.int32, (RW_PAIR, RW_PAIR), 0)
    ci = lax.broadcasted_iota(jnp.int32, (RW_PAIR, RW_PAIR), 1)
    strict = ci < ri
    incl = ci <= ri
    eye = jnp.where(ci == ri, 1.0, 0.0)

    def stack(t):
        return jnp.concatenate([jnp.where(first, t, 0.0), jnp.where(first, 0.0, t)], axis=0)

    def chunk_step(c, carry):
        c0 = pl.multiple_of(c * C, C)
        rows = pl.ds(c0, C)
        for j in range(n_pairs):
            sl = slice(j * RW_PAIR, (j + 1) * RW_PAIR)
            rts, khs = stack(rt_ref[rows, sl]), stack(kh_ref[rows, sl])
            ats, bhs = stack(at_ref[rows, sl]), stack(bh_ref[rows, sl])
            vs = stack(v_ref[rows, sl])
            g_last = gl_ref[pl.ds(pl.multiple_of(c0 + C - 8, 8), 8), sl][7:8]

            sc = _bdot_nt(jnp.concatenate([ats, rts], axis=0), jnp.concatenate([bhs, khs], axis=0))
            l_ab = jnp.where(strict, sc[:RW_PAIR, :RW_PAIR], 0.0)
            l_ak = jnp.where(strict, sc[:RW_PAIR, RW_PAIR:], 0.0)
            a_rb = jnp.where(incl, sc[RW_PAIR:, :RW_PAIR], 0.0)
            a_rk = jnp.where(incl, sc[RW_PAIR:, RW_PAIR:], 0.0)

            pw = l_ab
            tinv = eye + l_ab
            for _ in range(RW_INV_DOUBLINGS):
                pw = _bdot(pw, pw)
                tinv = tinv + _bdot(tinv, pw)

            lv = _bdot(l_ak, vs)
            taw = _bdot(tinv, jnp.concatenate([ats, lv], axis=1))
            ta, w_u = taw[:, :RW_PAIR], taw[:, RW_PAIR:]
            ar = _bdot(jnp.concatenate([a_rb, a_rk], axis=1),
                       jnp.concatenate([taw, jnp.concatenate([jnp.zeros_like(vs), vs], axis=1)], axis=0))
            qt = rts + ar[:, :RW_PAIR]
            y0 = ar[:, RW_PAIR:]
            g0 = _bdot_tn(vs, khs)

            s0 = s_ref[j]
            m1 = _bdot_nt(jnp.concatenate([qt, ta], axis=0), s0)
            y = m1[:RW_PAIR] + y0
            u = m1[RW_PAIR:] + w_u
            s_ref[j] = (s0 + g0 + _bdot_tn(u, bhs)) * g_last
            y_ref[rows, sl] = y[:C] + y[C:]
        return carry

    lax.fori_loop(0, tb // C, chunk_step, 0)

    y = y_ref[...]
    inv_n = 1.0 / RW_HEAD
    mean = head_sum(y) * inv_n
    yc = y - mean
    var = head_sum(yc * yc) * inv_n
    yn = yc * lax.rsqrt(var + GN_EPS) * lnw_ref[...] + lnb_ref[...]
    o_ref[0] = ((yn + bonus) * g).astype(o_ref.dtype)


def _rwkv7(p_rw, shift_mu, w0, w2, a0, a2, g2, k_k, k_a, r_k, lnx_w, lnx_b, B, S):
    cols = p_rw.shape[-1]
    width = (cols - DECAY_LORA - AAA_LORA - GATE_LORA) // 3
    tb = _pick_tile(S, 256)
    assert tb % RW_CHUNK == 0 and width % RW_PAIR == 0
    p3 = p_rw.reshape(B, S, cols)
    vec = lambda t: t.reshape(1, -1).astype(F32)
    zeros = jnp.zeros_like(w2)
    wa = jnp.concatenate([jnp.concatenate([w2, zeros], axis=1),
                          jnp.concatenate([jnp.zeros_like(a2), a2], axis=1)], axis=0).astype(BF16)
    blk = lambda b, i: (b, i, 0)
    prev_blk = lambda b, i: (b, jnp.maximum(i * (tb // 8) - 1, 0), 0)
    fixed = lambda b, i: (0, 0)
    wide = pltpu.VMEM((tb, width), F32)
    out = pl.pallas_call(
        functools.partial(_rwkv_body, tb=tb, width=width),
        grid=(B, S // tb),
        in_specs=[pl.BlockSpec((1, tb, cols), blk),
                  pl.BlockSpec((1, 8, cols), prev_blk),
                  pl.BlockSpec((1, cols), fixed)]
                 + [pl.BlockSpec((1, width), fixed)] * 7
                 + [pl.BlockSpec((DECAY_LORA + AAA_LORA, 2 * width), fixed),
                    pl.BlockSpec((GATE_LORA, width), fixed)],
        out_specs=pl.BlockSpec((1, tb, width), blk),
        out_shape=jax.ShapeDtypeStruct((B, S, width), BF16),
        scratch_shapes=[pltpu.VMEM((width // RW_PAIR, RW_PAIR, RW_PAIR), F32)] + [wide] * 7,
        compiler_params=_cparams("parallel", "arbitrary"),
        name="rwkv7_mix",
    )(p3, p3, vec(shift_mu), vec(w0), vec(a0), vec(k_k), vec(k_a), vec(r_k), vec(lnx_w), vec(lnx_b),
      wa, g2.astype(BF16))
    return out.reshape(B * S, width)


def _outproj_router_body(oda_ref, orw_ref, x_ref, wa_ref, wb_ref, nw_ref, rw_ref, rb_ref,
                         x1_ref, h_ref, idx_ref, gate_ref, rank_ref, cnt_ref, carry_ref):
    i = pl.program_id(0)

    @pl.when(i == 0)
    def _():
        carry_ref[...] = jnp.zeros(carry_ref.shape, F32)

    x1 = (x_ref[...] + jnp.dot(oda_ref[...], wa_ref[...], preferred_element_type=F32)
          + jnp.dot(orw_ref[...], wb_ref[...], preferred_element_type=F32))
    x1_ref[...] = x1
    h = (x1 * lax.rsqrt(jnp.mean(x1 * x1, axis=-1, keepdims=True) + NORM_EPS)) * nw_ref[...]
    h_ref[...] = h
    logits = jnp.dot(h, rw_ref[...], preferred_element_type=F32,
                     precision=lax.Precision.HIGHEST) + rb_ref[...]
    tm, n_exp = logits.shape
    lane = lax.broadcasted_iota(jnp.int32, logits.shape, 1).astype(F32)
    out_lane = lax.broadcasted_iota(jnp.int32, (tm, TOP_K), 1)

    vals, idxs = [], []
    rest = logits
    for _ in range(TOP_K):
        m = jnp.max(rest, axis=-1, keepdims=True)
        sel = jnp.min(jnp.where(rest == m, lane, float(n_exp)), axis=-1, keepdims=True)
        vals.append(m)
        idxs.append(sel)
        rest = jnp.where(lane == sel, -jnp.inf, rest)
    exps = [jnp.exp(v - vals[0]) for v in vals]
    denom = exps[0]
    for e in exps[1:]:
        denom = denom + e

    hot = [jnp.where(lane == s, 1.0, 0.0) for s in idxs]
    hot_all = hot[0]
    for t in hot[1:]:
        hot_all = hot_all + t
    tr = lax.broadcasted_iota(jnp.int32, (tm, tm), 0)
    tc = lax.broadcasted_iota(jnp.int32, (tm, tm), 1)
    before = jnp.dot(jnp.where(tc < tr, 1.0, 0.0).astype(BF16), hot_all.astype(BF16),
                     preferred_element_type=F32) + carry_ref[...]

    idx_out = jnp.zeros((tm, TOP_K), F32)
    gate_out = jnp.zeros((tm, TOP_K), F32)
    rank_out = jnp.zeros((tm, TOP_K), F32)
    for kk in range(TOP_K):
        rank_k = jnp.sum(hot[kk] * before, axis=-1, keepdims=True)
        idx_out = jnp.where(out_lane == kk, idxs[kk], idx_out)
        gate_out = jnp.where(out_lane == kk, exps[kk] / denom, gate_out)
        rank_out = jnp.where(out_lane == kk, rank_k, rank_out)
    idx_ref[...] = idx_out.astype(jnp.int32)
    gate_ref[...] = gate_out
    rank_ref[...] = rank_out.astype(jnp.int32)
    carry_ref[...] += jnp.sum(hot_all, axis=0, keepdims=True)
    cnt_ref[...] = carry_ref[...]


def _outproj_router(o_da, o_rw, x2, w_out, ffn_norm_w, router_w, router_b):
    T, D = x2.shape
    n_exp = router_w.shape[1]
    tm = _pick_tile(T, 512)
    wa = w_out[:DA_WIDTH].astype(BF16)
    wb = w_out[DA_WIDTH:].astype(BF16)
    row = lambda i: (i, 0)
    fixed = lambda i: (0, 0)
    return pl.pallas_call(
        _outproj_router_body,
        grid=(T // tm,),
        in_specs=[pl.BlockSpec((tm, DA_WIDTH), row),
                  pl.BlockSpec((tm, o_rw.shape[1]), row),
                  pl.BlockSpec((tm, D), row),
                  pl.BlockSpec(wa.shape, fixed),
                  pl.BlockSpec(wb.shape, fixed),
                  pl.BlockSpec((1, D), fixed),
                  pl.BlockSpec((D, n_exp), fixed),
                  pl.BlockSpec((1, n_exp), fixed)],
        out_specs=[pl.BlockSpec((tm, D), row),
                   pl.BlockSpec((tm, D), row),
                   pl.BlockSpec((tm, TOP_K), row),
                   pl.BlockSpec((tm, TOP_K), row),
                   pl.BlockSpec((tm, TOP_K), row),
                   pl.BlockSpec((1, n_exp), fixed)],
        out_shape=[jax.ShapeDtypeStruct((T, D), F32),
                   jax.ShapeDtypeStruct((T, D), F32),
                   jax.ShapeDtypeStruct((T, TOP_K), jnp.int32),
                   jax.ShapeDtypeStruct((T, TOP_K), F32),
                   jax.ShapeDtypeStruct((T, TOP_K), jnp.int32),
                   jax.ShapeDtypeStruct((1, n_exp), F32)],
        scratch_shapes=[pltpu.VMEM((1, n_exp), F32)],
        compiler_params=_cparams("arbitrary"),
        name="outproj_router",
    )(o_da, o_rw, x2, wa, wb, ffn_norm_w.reshape(1, D), router_w, router_b.reshape(1, n_exp))


MOE_ROWS = 256
DISPATCH_TOKENS = 256
DMA_IN_FLIGHT = 32


def _dispatch_body(dest_ref, h_hbm, init_hbm, xs_hbm, sem):
    del init_hbm
    i = pl.program_id(0)
    n_rows = DISPATCH_TOKENS * TOP_K

    def copy(n):
        tok = i * DISPATCH_TOKENS + n // TOP_K
        return pltpu.make_async_copy(h_hbm.at[pl.ds(tok, 1)], xs_hbm.at[pl.ds(dest_ref[0, 0, n], 1)], sem)

    def issue(n, carry):
        copy(n).start()

        @pl.when(n >= DMA_IN_FLIGHT)
        def _():
            copy(n - DMA_IN_FLIGHT).wait()
        return carry

    lax.fori_loop(0, n_rows, issue, 0)

    def drain(n, carry):
        copy(n).wait()
        return carry

    lax.fori_loop(n_rows - DMA_IN_FLIGHT, n_rows, drain, 0)


def _dispatch(h, dest, n_padded):
    T, D = h.shape
    n_blocks = T // DISPATCH_TOKENS
    dest3 = dest.reshape(n_blocks, 1, DISPATCH_TOKENS * TOP_K)
    return pl.pallas_call(
        _dispatch_body,
        grid=(n_blocks,),
        in_specs=[pl.BlockSpec((1, 1, DISPATCH_TOKENS * TOP_K), lambda i: (i, 0, 0), memory_space=pltpu.SMEM),
                  pl.BlockSpec(memory_space=pl.ANY),
                  pl.BlockSpec(memory_space=pl.ANY)],
        out_specs=pl.BlockSpec(memory_space=pl.ANY),
        out_shape=jax.ShapeDtypeStruct((n_padded, D), h.dtype),
        scratch_shapes=[pltpu.SemaphoreType.DMA(())],
        input_output_aliases={2: 0},
        compiler_params=_cparams("arbitrary"),
        name="moe_dispatch",
    )(dest3, h, jnp.zeros((n_padded, D), h.dtype))


def _experts_body(be_ref, nu_ref, xs_ref, wg_ref, wu_ref, bg_ref, bu_ref, w2_ref, b2_ref, y_ref):
    del be_ref
    i = pl.program_id(0)

    @pl.when(i < nu_ref[0])
    def _():
        x = xs_ref[...].astype(BF16)
        gate = jnp.dot(x, wg_ref[0], preferred_element_type=F32) + bg_ref[0]
        up = jnp.dot(x, wu_ref[0], preferred_element_type=F32) + bu_ref[0]
        gate = jnp.minimum(gate, SWIGLU_LIMIT)
        up = jnp.clip(up, -SWIGLU_LIMIT, SWIGLU_LIMIT)
        act = (up + 1.0) * gate * jax.nn.sigmoid(SWIGLU_ALPHA * gate)
        y_ref[...] = jnp.dot(act.astype(BF16), w2_ref[0], preferred_element_type=F32) + b2_ref[0]

    @pl.when(i >= nu_ref[0])
    def _():
        y_ref[...] = jnp.zeros(y_ref.shape, y_ref.dtype)


def _experts(xs, blk_expert, n_used, wg, wu, bg, bu, w2, b2):
    n_padded, D = xs.shape
    n_blocks = n_padded // MOE_ROWS
    d_ff = wg.shape[2]
    xrow = lambda i, be, nu: (jnp.minimum(i, nu[0] - 1), 0)
    yrow = lambda i, be, nu: (i, 0)
    wsel = lambda i, be, nu: (be[i], 0, 0)
    return pl.pallas_call(
        _experts_body,
        grid_spec=pltpu.PrefetchScalarGridSpec(
            num_scalar_prefetch=2,
            grid=(n_blocks,),
            in_specs=[pl.BlockSpec((MOE_ROWS, D), xrow),
                      pl.BlockSpec((1, D, d_ff), wsel),
                      pl.BlockSpec((1, D, d_ff), wsel),
                      pl.BlockSpec((1, 1, d_ff), wsel),
                      pl.BlockSpec((1, 1, d_ff), wsel),
                      pl.BlockSpec((1, d_ff, D), wsel),
                      pl.BlockSpec((1, 1, D), wsel)],
            out_specs=pl.BlockSpec((MOE_ROWS, D), yrow)),
        out_shape=jax.ShapeDtypeStruct((n_padded, D), F32),
        compiler_params=_cparams("arbitrary"),
        name="moe_experts",
    )(blk_expert, n_used, xs, wg, wu, bg, bu, w2, b2)


def _combine_body(dest_ref, ys_hbm, x1_ref, gate_ref, fw_ref, o_ref, buf_ref, sem):
    n_rows = DISPATCH_TOKENS * TOP_K

    def copy(n):
        return pltpu.make_async_copy(ys_hbm.at[pl.ds(dest_ref[0, 0, n], 1)],
                                     buf_ref.at[n % TOP_K, pl.ds(n // TOP_K, 1)], sem)

    def issue(n, carry):
        copy(n).start()
        return carry

    def drain(n, carry):
        copy(n).wait()
        return carry

    lax.fori_loop(0, n_rows, issue, 0)
    lax.fori_loop(0, n_rows, drain, 0)

    gates = gate_ref[...]
    acc = x1_ref[...]
    moe = gates[:, 0:1] * buf_ref[0]
    for kk in range(1, TOP_K):
        moe = moe + gates[:, kk:kk + 1] * buf_ref[kk]
    x2 = acc + moe
    o_ref[...] = (x2 * lax.rsqrt(jnp.mean(x2 * x2, axis=-1, keepdims=True) + NORM_EPS)) * fw_ref[...]


def _combine(ys, dest, x1, gates, final_w):
    T, D = x1.shape
    n_blocks = T // DISPATCH_TOKENS
    dest3 = dest.reshape(n_blocks, 1, DISPATCH_TOKENS * TOP_K)
    row = lambda i: (i, 0)
    return pl.pallas_call(
        _combine_body,
        grid=(n_blocks,),
        in_specs=[pl.BlockSpec((1, 1, DISPATCH_TOKENS * TOP_K), lambda i: (i, 0, 0), memory_space=pltpu.SMEM),
                  pl.BlockSpec(memory_space=pl.ANY),
                  pl.BlockSpec((DISPATCH_TOKENS, D), row),
                  pl.BlockSpec((DISPATCH_TOKENS, TOP_K), row),
                  pl.BlockSpec((1, D), lambda i: (0, 0))],
        out_specs=pl.BlockSpec((DISPATCH_TOKENS, D), row),
        out_shape=jax.ShapeDtypeStruct((T, D), F32),
        scratch_shapes=[pltpu.VMEM((TOP_K, DISPATCH_TOKENS, D), F32), pltpu.SemaphoreType.DMA(())],
        compiler_params=_cparams("arbitrary"),
        name="moe_combine",
    )(dest3, ys, x1, gates, final_w.reshape(1, D))


def _moe(h, x1, idx, gates, rank, counts, mlp1_w, mlp1_b, mlp2_w, mlp2_b, final_w):
    T, D = h.shape
    n_exp = mlp1_w.shape[0]
    assert T % DISPATCH_TOKENS == 0
    n_padded = T * TOP_K + n_exp * MOE_ROWS
    n_blocks = n_padded // MOE_ROWS
    counts = counts.reshape(n_exp).astype(jnp.int32)
    padded = (counts + MOE_ROWS - 1) // MOE_ROWS * MOE_ROWS
    pend = jnp.cumsum(padded)
    pstart = pend - padded
    dest = (pstart[idx] + rank).astype(jnp.int32)
    blk_expert = jnp.minimum(
        jnp.searchsorted(pend, jnp.arange(n_blocks, dtype=jnp.int32) * MOE_ROWS, side="right"),
        n_exp - 1).astype(jnp.int32)
    n_used = (pend[-1:] // MOE_ROWS).astype(jnp.int32)

    wg = mlp1_w[:, :, 0::2].astype(BF16)
    wu = mlp1_w[:, :, 1::2].astype(BF16)
    bg = mlp1_b[:, None, 0::2]
    bu = mlp1_b[:, None, 1::2]
    xs = _dispatch(h, dest, n_padded)
    ys = _experts(xs, blk_expert, n_used, wg, wu, bg, bu, mlp2_w.astype(BF16), mlp2_b[:, None, :])
    return _combine(ys, dest, x1, gates, final_w)


def kernel(x, positions, attn_norm_w, w_in, shift_mu, lambda_q1, lambda_k1, lambda_q2, lambda_k2, subln_w,
           rw_w0, rw_w2, rw_a0, rw_a2, rw_g2, rw_k_k, rw_k_a, rw_r_k, rw_lnx_w, rw_lnx_b, w_out, ffn_norm_w,
           router_w, router_b, mlp1_w, mlp1_b, mlp2_w, mlp2_b, final_norm_w):
    B, S, D = x.shape
    assert attn_norm_w.shape[0] == 1, "single-layer stack"
    l = LAYER_INDEX
    x2 = x.reshape(B * S, D)
    cos_t, sin_t = _rope_tables(positions)
    q, k, v, p_rw = _input_projection(x2, attn_norm_w[l], w_in[l], cos_t, sin_t)
    o_da = _diff_attention(q, k, v, lambda_q1[l], lambda_k1[l], lambda_q2[l], lambda_k2[l], subln_w[l], B, S)
    o_rw = _rwkv7(p_rw, shift_mu[l], rw_w0[l], rw_w2[l], rw_a0[l], rw_a2[l], rw_g2[l],
                  rw_k_k[l], rw_k_a[l], rw_r_k[l], rw_lnx_w[l], rw_lnx_b[l], B, S)
    x1, h, idx, gates, rank, counts = _outproj_router(o_da, o_rw, x2, w_out[l], ffn_norm_w[l],
                                                      router_w[l], router_b[l])
    out = _moe(h, x1, idx, gates, rank, counts, mlp1_w[l], mlp1_b[l], mlp2_w[l], mlp2_b[l], final_norm_w)
    return out.reshape(B, S, D)
```

```python
import functools
import math

import jax
import jax.numpy as jnp
from jax import lax
from jax.experimental import pallas as pl
from jax.experimental.pallas import tpu as pltpu

F32 = jnp.float32
BF16 = jnp.bfloat16

V7X_LANES = 128
V7X_VMEM_BYTES = 64 * 1024 * 1024
VMEM_LIMIT_BYTES = 56 * 1024 * 1024

DA_HEADS = 4
DA_HEAD_DIM = 64
DA_V_DIM = 2 * DA_HEAD_DIM
DA_WIDTH = DA_HEADS * DA_V_DIM
RW_HEAD = 64
DECAY_LORA = 64
AAA_LORA = 64
GATE_LORA = 128
ROPE_THETA = 10000.0
N_EXPERTS = 32
TOP_K = 4
SWIGLU_ALPHA = 1.702
SWIGLU_LIMIT = 7.0
NORM_EPS = 1e-5
GN_EPS = 64e-5
LAYER_INDEX = 0
LAM_INIT = 0.8 - 0.6 * math.exp(-0.3 * LAYER_INDEX)


def _cparams(*semantics):
    return pltpu.CompilerParams(dimension_semantics=semantics, vmem_limit_bytes=VMEM_LIMIT_BYTES)


def _pick_tile(n, want):
    t = min(n, want)
    while n % t:
        t //= 2
    return t


def _inproj_body(x_ref, nw_ref, w_ref, cos_ref, sin_ref, q_ref, k_ref, v_ref, prw_ref):
    x = x_ref[...]
    ms = jnp.mean(x * x, axis=-1, keepdims=True)
    h = (x * lax.rsqrt(ms + NORM_EPS)) * nw_ref[...]
    p = jnp.dot(h.astype(BF16), w_ref[...], preferred_element_type=F32)
    cos = cos_ref[...]
    sin = sin_ref[...]
    lane = lax.broadcasted_iota(jnp.int32, cos.shape, 1)
    first_half = (lane % DA_HEAD_DIM) < (DA_HEAD_DIM // 2)
    scale = DA_HEAD_DIM ** -0.5

    def rope(t):
        partner = jnp.where(first_half,
                            pltpu.roll(t, V7X_LANES - DA_HEAD_DIM // 2, 1),
                            pltpu.roll(t, DA_HEAD_DIM // 2, 1))
        return t * cos + partner * sin

    for hh in range(DA_HEADS):
        sl = slice(hh * DA_V_DIM, (hh + 1) * DA_V_DIM)
        q_ref[:, sl] = (rope(p[:, sl]) * scale).astype(q_ref.dtype)
        ksl = slice(DA_WIDTH + hh * DA_V_DIM, DA_WIDTH + (hh + 1) * DA_V_DIM)
        k_ref[:, sl] = rope(p[:, ksl]).astype(k_ref.dtype)
    v_ref[...] = p[:, 2 * DA_WIDTH:3 * DA_WIDTH].astype(v_ref.dtype)
    prw_ref[...] = p[:, 3 * DA_WIDTH:]


def _input_projection(x2, norm_w, w_in, cos_t, sin_t):
    T, D = x2.shape
    n_cols = w_in.shape[1]
    rw_cols = n_cols - 3 * DA_WIDTH
    tm = _pick_tile(T, 512)
    row = lambda i: (i, 0)
    fixed = lambda i: (0, 0)
    return pl.pallas_call(
        _inproj_body,
        grid=(T // tm,),
        in_specs=[pl.BlockSpec((tm, D), row),
                  pl.BlockSpec((1, D), fixed),
                  pl.BlockSpec((D, n_cols), fixed),
                  pl.BlockSpec((tm, DA_V_DIM), row),
                  pl.BlockSpec((tm, DA_V_DIM), row)],
        out_specs=[pl.BlockSpec((tm, DA_WIDTH), row),
                   pl.BlockSpec((tm, DA_WIDTH), row),
                   pl.BlockSpec((tm, DA_WIDTH), row),
                   pl.BlockSpec((tm, rw_cols), row)],
        out_shape=[jax.ShapeDtypeStruct((T, DA_WIDTH), BF16),
                   jax.ShapeDtypeStruct((T, DA_WIDTH), BF16),
                   jax.ShapeDtypeStruct((T, DA_WIDTH), BF16),
                   jax.ShapeDtypeStruct((T, rw_cols), F32)],
        compiler_params=_cparams("parallel"),
        name="input_projection",
    )(x2, norm_w.reshape(1, D), w_in.astype(BF16), cos_t, sin_t)


def _rope_tables(positions):
    half = DA_HEAD_DIM // 2
    inv = ROPE_THETA ** (-jnp.arange(0, DA_HEAD_DIM, 2, dtype=F32) / DA_HEAD_DIM)
    ang = positions.reshape(-1).astype(F32)[:, None] * inv
    cos, sin = jnp.cos(ang), jnp.sin(ang)
    cos_t = jnp.tile(cos, (1, DA_V_DIM // half))
    sin_t = jnp.tile(jnp.concatenate([-sin, sin], axis=-1), (1, DA_V_DIM // DA_HEAD_DIM))
    return cos_t, sin_t


def _diff_attn_body(q_ref, k_ref, v_ref, lq1_ref, lk1_ref, lq2_ref, lk2_ref, sw_ref, o_ref,
                    m_ref, l_ref, acc_ref, *, tq, tk):
    qi = pl.program_id(2)
    q = q_ref[0]
    lane = lax.broadcasted_iota(jnp.int32, q.shape, 1)
    zero = jnp.zeros_like(q)
    q_sub = (jnp.where(lane < DA_HEAD_DIM, q, zero), jnp.where(lane >= DA_HEAD_DIM, q, zero))

    m_ref[...] = jnp.full(m_ref.shape, -jnp.inf, F32)
    l_ref[...] = jnp.zeros(l_ref.shape, F32)
    acc_ref[...] = jnp.zeros(acc_ref.shape, F32)

    def step(j, masked):
        ks = pl.multiple_of(j * tk, tk)
        kb = k_ref[0, pl.ds(ks, tk), :]
        vb = v_ref[0, pl.ds(ks, tk), :]
        for c in range(2):
            s = lax.dot_general(q_sub[c], kb, (((1,), (1,)), ((), ())), preferred_element_type=F32)
            if masked:
                rows = lax.broadcasted_iota(jnp.int32, s.shape, 0)
                cols = lax.broadcasted_iota(jnp.int32, s.shape, 1)
                s = jnp.where(cols <= rows, s, -jnp.inf)
            m_prev = m_ref[c]
            m_new = jnp.maximum(m_prev, jnp.max(s, axis=-1, keepdims=True))
            alpha = jnp.exp(m_prev - m_new)
            p = jnp.exp(s - m_new)
            l_ref[c] = alpha * l_ref[c] + jnp.sum(p, axis=-1, keepdims=True)
            acc_ref[c] = alpha * acc_ref[c] + jnp.dot(p.astype(vb.dtype), vb, preferred_element_type=F32)
            m_ref[c] = m_new

    def full_step(j, carry):
        step(j, False)
        return carry

    lax.fori_loop(0, qi * (tq // tk), full_step, 0)
    for d in range(tq // tk):
        step(qi * (tq // tk) + d, True)

    lam = (jnp.exp(jnp.sum(lq1_ref[...] * lk1_ref[...], axis=-1, keepdims=True))
           - jnp.exp(jnp.sum(lq2_ref[...] * lk2_ref[...], axis=-1, keepdims=True)) + LAM_INIT)
    o = acc_ref[0] / l_ref[0] - lam * (acc_ref[1] / l_ref[1])
    o = o * lax.rsqrt(jnp.mean(o * o, axis=-1, keepdims=True) + NORM_EPS)
    o_ref[0] = (o * sw_ref[...] * (1.0 - LAM_INIT)).astype(o_ref.dtype)


def _diff_attention(q, k, v, lq1, lk1, lq2, lk2, subln_w, B, S):
    tq = tk = _pick_tile(S, 512)
    q3, k3, v3 = (t.reshape(B, S, DA_WIDTH) for t in (q, k, v))
    vec = lambda a: a.reshape(1, -1).astype(F32)
    blk = lambda b, h, i: (b, i, h)
    seq = lambda b, h, i: (b, 0, h)
    fixed = lambda b, h, i: (0, 0)
    out = pl.pallas_call(
        functools.partial(_diff_attn_body, tq=tq, tk=tk),
        grid=(B, DA_HEADS, S // tq),
        in_specs=[pl.BlockSpec((1, tq, DA_V_DIM), blk),
                  pl.BlockSpec((1, S, DA_V_DIM), seq),
                  pl.BlockSpec((1, S, DA_V_DIM), seq),
                  pl.BlockSpec((1, DA_HEAD_DIM), fixed),
                  pl.BlockSpec((1, DA_HEAD_DIM), fixed),
                  pl.BlockSpec((1, DA_HEAD_DIM), fixed),
                  pl.BlockSpec((1, DA_HEAD_DIM), fixed),
                  pl.BlockSpec((1, DA_V_DIM), fixed)],
        out_specs=pl.BlockSpec((1, tq, DA_V_DIM), blk),
        out_shape=jax.ShapeDtypeStruct((B, S, DA_WIDTH), BF16),
        scratch_shapes=[pltpu.VMEM((2, tq, 1), F32),
                        pltpu.VMEM((2, tq, 1), F32),
                        pltpu.VMEM((2, tq, DA_V_DIM), F32)],
        compiler_params=_cparams("parallel", "parallel", "arbitrary"),
        name="diff_attention",
    )(q3, k3, v3, vec(lq1), vec(lk1), vec(lq2), vec(lk2), vec(subln_w))
    return out.reshape(B * S, DA_WIDTH)


RW_CHUNK = 64
RW_PAIR = 2 * RW_HEAD
RW_INV_DOUBLINGS = 5


def _bdot(a, b):
    return jnp.dot(a.astype(BF16), b.astype(BF16), preferred_element_type=F32)


def _bdot_nt(a, b):
    return lax.dot_general(a.astype(BF16), b.astype(BF16), (((1,), (1,)), ((), ())),
                           preferred_element_type=F32)


def _bdot_tn(a, b):
    return lax.dot_general(a.astype(BF16), b.astype(BF16), (((0,), (0,)), ((), ())),
                           preferred_element_type=F32)


def _split3_dot(m, x):
    mb = m.astype(BF16)
    x0 = x.astype(BF16)
    r1 = x - x0.astype(F32)
    x1 = r1.astype(BF16)
    x2 = (r1 - x1.astype(F32)).astype(BF16)
    dot = lambda t: jnp.dot(mb, t, preferred_element_type=F32)
    return dot(x0) + dot(x1) + dot(x2)


def _rwkv_body(p_ref, prev_ref, mu_ref, w0_ref, a0_ref, kk_ref, ka_ref, rk_ref, lnw_ref, lnb_ref,
               wa_ref, g2_ref, o_ref,
               s_ref, rt_ref, kh_ref, at_ref, bh_ref, v_ref, gl_ref, y_ref, *, tb, width):
    i = pl.program_id(1)
    n_pairs = width // RW_PAIR
    C = RW_CHUNK

    @pl.when(i == 0)
    def _():
        s_ref[...] = jnp.zeros(s_ref.shape, F32)

    p = p_ref[0]
    row = lax.broadcasted_iota(jnp.int32, p.shape, 0)
    last_prev = jnp.where(i == 0, 0.0, prev_ref[0, 7:8, :])
    prev = jnp.where(row == 0, last_prev, pltpu.roll(p, 1, 0))
    ps = p + mu_ref[...] * (prev - p)
    r = ps[:, 0:width]
    k = ps[:, width:2 * width]
    v = ps[:, 2 * width:3 * width]
    c3 = 3 * width
    wa_in = ps[:, c3:c3 + DECAY_LORA + AAA_LORA]
    lane = lax.broadcasted_iota(jnp.int32, wa_in.shape, 1)
    wa_in = jnp.where(lane < DECAY_LORA, jnp.tanh(wa_in), wa_in)
    wa = _bdot(wa_in, wa_ref[...])
    gd = ps[:, c3 + DECAY_LORA + AAA_LORA:]
    g = _bdot(jax.nn.sigmoid(gd), g2_ref[...])

    wl = w0_ref[...] + wa[:, :width]
    softplus_neg = jnp.maximum(-wl, 0.0) + jnp.log1p(jnp.exp(-jnp.abs(wl)))
    logw = -jnp.exp(-softplus_neg - 0.5)
    a = jax.nn.sigmoid(a0_ref[...] + wa[:, width:])

    hr = lax.broadcasted_iota(jnp.int32, (width, width), 0) // RW_HEAD
    hc = lax.broadcasted_iota(jnp.int32, (width, width), 1) // RW_HEAD
    head_ones = jnp.where(hr == hc, 1.0, 0.0).astype(BF16)
    head_sum = lambda t: jnp.dot(t.astype(BF16), head_ones, preferred_element_type=F32)

    kk = k * kk_ref[...]
    kk = kk / jnp.maximum(jnp.sqrt(head_sum(kk * kk)), 1e-12)
    k2 = k * (1.0 + (a - 1.0) * ka_ref[...])
    bonus = head_sum(r * k2 * rk_ref[...]) * v

    tr = lax.broadcasted_iota(jnp.int32, (tb, tb), 0)
    tc = lax.broadcasted_iota(jnp.int32, (tb, tb), 1)
    chunk_tril = jnp.where((tr // C == tc // C) & (tc <= tr), 1.0, 0.0)
    cum = _split3_dot(chunk_tril, logw)
    g_inc = jnp.exp(cum)
    g_inv = jnp.exp(-cum)
    g_exc = jnp.exp(cum - logw)
    rt_ref[...] = r * g_inc
    kh_ref[...] = k2 * g_inv
    at_ref[...] = -kk * g_exc
    bh_ref[...] = kk * a * g_inv
    v_ref[...] = v
    gl_ref[...] = g_inc

    lane_p = lax.broadcasted_iota(jnp.int32, (C, RW_PAIR), 1)
    first = lane_p < RW_HEAD
    ri = lax.broadcasted_iota(jnp.int32, (RW_PAIR, RW_PAIR), 0)
    ci = lax.broadcasted_iota(jnp.int32, (RW_PAIR, RW_PAIR), 1)
    strict = ci < ri
    incl = ci <= ri
    eye = jnp.where(ci == ri, 1.0, 0.0)

    def stack(t):
        return jnp.concatenate([jnp.where(first, t, 0.0), jnp.where(first, 0.0, t)], axis=0)

    def chunk_step(c, carry):
        c0 = pl.multiple_of(c * C, C)
        rows = pl.ds(c0, C)
        for j in range(n_pairs):
            sl = slice(j * RW_PAIR, (j + 1) * RW_PAIR)
            rts, khs = stack(rt_ref[rows, sl]), stack(kh_ref[rows, sl])
            ats, bhs = stack(at_ref[rows, sl]), stack(bh_ref[rows, sl])
            vs = stack(v_ref[rows, sl])
            g_last = gl_ref[pl.ds(pl.multiple_of(c0 + C - 8, 8), 8), sl][7:8]

            sc = _bdot_nt(jnp.concatenate([ats, rts], axis=0), jnp.concatenate([bhs, khs], axis=0))
            l_ab = jnp.where(strict, sc[:RW_PAIR, :RW_PAIR], 0.0)
            l_ak = jnp.where(strict, sc[:RW_PAIR, RW_PAIR:], 0.0)
            a_rb = jnp.where(incl, sc[RW_PAIR:, :RW_PAIR], 0.0)
            a_rk = jnp.where(incl, sc[RW_PAIR:, RW_PAIR:], 0.0)

            pw = l_ab
            tinv = eye + l_ab
            for _ in range(RW_INV_DOUBLINGS):
                pw = _bdot(pw, pw)
                tinv = tinv + _bdot(tinv, pw)

            lv = _bdot(l_ak, vs)
            taw = _bdot(tinv, jnp.concatenate([ats, lv], axis=1))
            ta, w_u = taw[:, :RW_PAIR], taw[:, RW_PAIR:]
            ar = _bdot(jnp.concatenate([a_rb, a_rk], axis=1),
                       jnp.concatenate([taw, jnp.concatenate([jnp.zeros_like(vs), vs], axis=1)], axis=0))
            qt = rts + ar[:, :RW_PAIR]
            y0 = ar[:, RW_PAIR:]
            g0 = _bdot_tn(vs, khs)

            s0 = s_ref[j]
            m1 = _bdot_nt(jnp.concatenate([qt, ta], axis=0), s0)
            y = m1[:RW_PAIR] + y0
            u = m1[RW_PAIR:] + w_u
            s_ref[j] = (s0 + g0 + _bdot_tn(u, bhs)) * g_last
            y_ref[rows, sl] = y[:C] + y[C:]
        return carry

    lax.fori_loop(0, tb // C, chunk_step, 0)

    y = y_ref[...]
    inv_n = 1.0 / RW_HEAD
    mean = head_sum(y) * inv_n
    yc = y - mean
    var = head_sum(yc * yc) * inv_n
    yn = yc * lax.rsqrt(var + GN_EPS) * lnw_ref[...] + lnb_ref[...]
    o_ref[0] = ((yn + bonus) * g).astype(o_ref.dtype)


def _rwkv7(p_rw, shift_mu, w0, w2, a0, a2, g2, k_k, k_a, r_k, lnx_w, lnx_b, B, S):
    cols = p_rw.shape[-1]
    width = (cols - DECAY_LORA - AAA_LORA - GATE_LORA) // 3
    tb = _pick_tile(S, 256)
    assert tb % RW_CHUNK == 0 and width % RW_PAIR == 0
    p3 = p_rw.reshape(B, S, cols)
    vec = lambda t: t.reshape(1, -1).astype(F32)
    zeros = jnp.zeros_like(w2)
    wa = jnp.concatenate([jnp.concatenate([w2, zeros], axis=1),
                          jnp.concatenate([jnp.zeros_like(a2), a2], axis=1)], axis=0).astype(BF16)
    blk = lambda b, i: (b, i, 0)
    prev_blk = lambda b, i: (b, jnp.maximum(i * (tb // 8) - 1, 0), 0)
    fixed = lambda b, i: (0, 0)
    wide = pltpu.VMEM((tb, width), F32)
    out = pl.pallas_call(
        functools.partial(_rwkv_body, tb=tb, width=width),
        grid=(B, S // tb),
        in_specs=[pl.BlockSpec((1, tb, cols), blk),
                  pl.BlockSpec((1, 8, cols), prev_blk),
                  pl.BlockSpec((1, cols), fixed)]
                 + [pl.BlockSpec((1, width), fixed)] * 7
                 + [pl.BlockSpec((DECAY_LORA + AAA_LORA, 2 * width), fixed),
                    pl.BlockSpec((GATE_LORA, width), fixed)],
        out_specs=pl.BlockSpec((1, tb, width), blk),
        out_shape=jax.ShapeDtypeStruct((B, S, width), BF16),
        scratch_shapes=[pltpu.VMEM((width // RW_PAIR, RW_PAIR, RW_PAIR), F32)] + [wide] * 7,
        compiler_params=_cparams("parallel", "arbitrary"),
        name="rwkv7_mix",
    )(p3, p3, vec(shift_mu), vec(w0), vec(a0), vec(k_k), vec(k_a), vec(r_k), vec(lnx_w), vec(lnx_b),
      wa, g2.astype(BF16))
    return out.reshape(B * S, width)


def _outproj_router_body(oda_ref, orw_ref, x_ref, wa_ref, wb_ref, nw_ref, rw_ref, rb_ref,
                         x1_ref, h_ref, idx_ref, gate_ref, rank_ref, cnt_ref, carry_ref):
    i = pl.program_id(0)

    @pl.when(i == 0)
    def _():
        carry_ref[...] = jnp.zeros(carry_ref.shape, F32)

    x1 = (x_ref[...] + jnp.dot(oda_ref[...], wa_ref[...], preferred_element_type=F32)
          + jnp.dot(orw_ref[...], wb_ref[...], preferred_element_type=F32))
    x1_ref[...] = x1
    h = (x1 * lax.rsqrt(jnp.mean(x1 * x1, axis=-1, keepdims=True) + NORM_EPS)) * nw_ref[...]
    h_ref[...] = h
    logits = jnp.dot(h, rw_ref[...], preferred_element_type=F32,
                     precision=lax.Precision.HIGHEST) + rb_ref[...]
    tm, n_exp = logits.shape
    lane = lax.broadcasted_iota(jnp.int32, logits.shape, 1).astype(F32)
    out_lane = lax.broadcasted_iota(jnp.int32, (tm, TOP_K), 1)

    vals, idxs = [], []
    rest = logits
    for _ in range(TOP_K):
        m = jnp.max(rest, axis=-1, keepdims=True)
        sel = jnp.min(jnp.where(rest == m, lane, float(n_exp)), axis=-1, keepdims=True)
        vals.append(m)
        idxs.append(sel)
        rest = jnp.where(lane == sel, -jnp.inf, rest)
    exps = [jnp.exp(v - vals[0]) for v in vals]
    denom = exps[0]
    for e in exps[1:]:
        denom = denom + e

    hot = [jnp.where(lane == s, 1.0, 0.0) for s in idxs]
    hot_all = hot[0]
    for t in hot[1:]:
        hot_all = hot_all + t
    tr = lax.broadcasted_iota(jnp.int32, (tm, tm), 0)
    tc = lax.broadcasted_iota(jnp.int32, (tm, tm), 1)
    before = jnp.dot(jnp.where(tc < tr, 1.0, 0.0).astype(BF16), hot_all.astype(BF16),
                     preferred_element_type=F32) + carry_ref[...]

    idx_out = jnp.zeros((tm, TOP_K), F32)
    gate_out = jnp.zeros((tm, TOP_K), F32)
    rank_out = jnp.zeros((tm, TOP_K), F32)
    for kk in range(TOP_K):
        rank_k = jnp.sum(hot[kk] * before, axis=-1, keepdims=True)
        idx_out = jnp.where(out_lane == kk, idxs[kk], idx_out)
        gate_out = jnp.where(out_lane == kk, exps[kk] / denom, gate_out)
        rank_out = jnp.where(out_lane == kk, rank_k, rank_out)
    idx_ref[...] = idx_out.astype(jnp.int32)
    gate_ref[...] = gate_out
    rank_ref[...] = rank_out.astype(jnp.int32)
    carry_ref[...] += jnp.sum(hot_all, axis=0, keepdims=True)
    cnt_ref[...] = carry_ref[...]


def _outproj_router(o_da, o_rw, x2, w_out, ffn_norm_w, router_w, router_b):
    T, D = x2.shape
    n_exp = router_w.shape[1]
    tm = _pick_tile(T, 512)
    wa = w_out[:DA_WIDTH].astype(BF16)
    wb = w_out[DA_WIDTH:].astype(BF16)
    row = lambda i: (i, 0)
    fixed = lambda i: (0, 0)
    return pl.pallas_call(
        _outproj_router_body,
        grid=(T // tm,),
        in_specs=[pl.BlockSpec((tm, DA_WIDTH), row),
                  pl.BlockSpec((tm, o_rw.shape[1]), row),
                  pl.BlockSpec((tm, D), row),
                  pl.BlockSpec(wa.shape, fixed),
                  pl.BlockSpec(wb.shape, fixed),
                  pl.BlockSpec((1, D), fixed),
                  pl.BlockSpec((D, n_exp), fixed),
                  pl.BlockSpec((1, n_exp), fixed)],
        out_specs=[pl.BlockSpec((tm, D), row),
                   pl.BlockSpec((tm, D), row),
                   pl.BlockSpec((tm, TOP_K), row),
                   pl.BlockSpec((tm, TOP_K), row),
                   pl.BlockSpec((tm, TOP_K), row),
                   pl.BlockSpec((1, n_exp), fixed)],
        out_shape=[jax.ShapeDtypeStruct((T, D), F32),
                   jax.ShapeDtypeStruct((T, D), F32),
                   jax.ShapeDtypeStruct((T, TOP_K), jnp.int32),
                   jax.ShapeDtypeStruct((T, TOP_K), F32),
                   jax.ShapeDtypeStruct((T, TOP_K), jnp.int32),
                   jax.ShapeDtypeStruct((1, n_exp), F32)],
        scratch_shapes=[pltpu.VMEM((1, n_exp), F32)],
        compiler_params=_cparams("arbitrary"),
        name="outproj_router",
    )(o_da, o_rw, x2, wa, wb, ffn_norm_w.reshape(1, D), router_w, router_b.reshape(1, n_exp))


MOE_ROWS = 256
DISPATCH_TOKENS = 256
ISSUE_UNROLL = 8
MXU_TILE = 256


def _dispatch_body(dest_ref, h_ref, init_hbm, xs_hbm, sem):
    del init_hbm

    def issue(t, carry):
        src = h_ref.at[pl.ds(t, 1)]
        for kk in range(TOP_K):
            pltpu.make_async_copy(src, xs_hbm.at[pl.ds(dest_ref[0, 0, t * TOP_K + kk], 1)], sem).start()
        return carry

    lax.fori_loop(0, DISPATCH_TOKENS, issue, 0, unroll=ISSUE_UNROLL)
    for _ in range(TOP_K):
        pltpu.make_async_copy(h_ref, xs_hbm.at[pl.ds(0, DISPATCH_TOKENS)], sem).wait()


def _dispatch(h, dest, n_padded):
    T, D = h.shape
    n_blocks = T // DISPATCH_TOKENS
    dest3 = dest.reshape(n_blocks, 1, DISPATCH_TOKENS * TOP_K)
    return pl.pallas_call(
        _dispatch_body,
        grid=(n_blocks,),
        in_specs=[pl.BlockSpec((1, 1, DISPATCH_TOKENS * TOP_K), lambda i: (i, 0, 0), memory_space=pltpu.SMEM),
                  pl.BlockSpec((DISPATCH_TOKENS, D), lambda i: (i, 0)),
                  pl.BlockSpec(memory_space=pl.ANY)],
        out_specs=pl.BlockSpec(memory_space=pl.ANY),
        out_shape=jax.ShapeDtypeStruct((n_padded, D), h.dtype),
        scratch_shapes=[pltpu.SemaphoreType.DMA(())],
        input_output_aliases={2: 0},
        compiler_params=_cparams("arbitrary"),
        name="moe_dispatch",
    )(dest3, h, jnp.zeros((n_padded, D), h.dtype))


def _mlp1_regroup_body(w_ref, o_ref):
    half = MXU_TILE // 2
    r = lax.broadcasted_iota(jnp.int32, (MXU_TILE, MXU_TILE), 0)
    c = lax.broadcasted_iota(jnp.int32, (MXU_TILE, MXU_TILE), 1)
    perm = jnp.where(c == (r >> 1) + (r & 1) * half, 1.0, 0.0).astype(BF16)
    for g in range(w_ref.shape[2] // MXU_TILE):
        sl = slice(g * MXU_TILE, (g + 1) * MXU_TILE)
        o_ref[0, :, sl] = jnp.dot(w_ref[0, :, sl].astype(BF16), perm,
                                  preferred_element_type=F32).astype(o_ref.dtype)


def _mlp1_regroup(mlp1_w):
    n_exp, D, cols = mlp1_w.shape
    tc = _pick_tile(cols, 2 * MXU_TILE)
    blk = lambda e, j: (e, 0, j)
    return pl.pallas_call(
        _mlp1_regroup_body,
        grid=(n_exp, cols // tc),
        in_specs=[pl.BlockSpec((1, D, tc), blk)],
        out_specs=pl.BlockSpec((1, D, tc), blk),
        out_shape=jax.ShapeDtypeStruct(mlp1_w.shape, BF16),
        compiler_params=_cparams("parallel", "parallel"),
        name="mlp1_regroup",
    )(mlp1_w)


def _experts_body(be_ref, nu_ref, xs_ref, w1_ref, b1_ref, w2_ref, b2_ref, y_ref):
    del be_ref
    i = pl.program_id(0)
    half = MXU_TILE // 2

    @pl.when(i < nu_ref[0])
    def _():
        x = xs_ref[...].astype(BF16)
        gu = jnp.dot(x, w1_ref[0], preferred_element_type=F32) + b1_ref[0]
        acts = []
        for g in range(gu.shape[1] // MXU_TILE):
            gate = jnp.minimum(gu[:, g * MXU_TILE:g * MXU_TILE + half], SWIGLU_LIMIT)
            up = jnp.clip(gu[:, g * MXU_TILE + half:(g + 1) * MXU_TILE], -SWIGLU_LIMIT, SWIGLU_LIMIT)
            acts.append(((up + 1.0) * gate * jax.nn.sigmoid(SWIGLU_ALPHA * gate)).astype(BF16))
        act = jnp.concatenate(acts, axis=1)
        y_ref[...] = jnp.dot(act, w2_ref[0], preferred_element_type=F32) + b2_ref[0]

    @pl.when(i >= nu_ref[0])
    def _():
        y_ref[...] = jnp.zeros(y_ref.shape, y_ref.dtype)


def _experts(xs, blk_expert, n_used, w1, b1, w2, b2):
    n_padded, D = xs.shape
    n_blocks = n_padded // MOE_ROWS
    gu_cols = w1.shape[2]
    xrow = lambda i, be, nu: (jnp.minimum(i, nu[0] - 1), 0)
    yrow = lambda i, be, nu: (i, 0)
    wsel = lambda i, be, nu: (be[i], 0, 0)
    return pl.pallas_call(
        _experts_body,
        grid_spec=pltpu.PrefetchScalarGridSpec(
            num_scalar_prefetch=2,
            grid=(n_blocks,),
            in_specs=[pl.BlockSpec((MOE_ROWS, D), xrow),
                      pl.BlockSpec((1, D, gu_cols), wsel),
                      pl.BlockSpec((1, 1, gu_cols), wsel),
                      pl.BlockSpec((1, gu_cols // 2, D), wsel),
                      pl.BlockSpec((1, 1, D), wsel)],
            out_specs=pl.BlockSpec((MOE_ROWS, D), yrow)),
        out_shape=jax.ShapeDtypeStruct((n_padded, D), F32),
        compiler_params=_cparams("arbitrary"),
        name="moe_experts",
    )(blk_expert, n_used, xs, w1, b1, w2, b2)


def _combine_body(dest_ref, next_ref, ys_hbm, x1_ref, gate_ref, fw_ref, o_ref, buf_ref, sems):
    i = pl.program_id(0)
    slot = i % 2

    def gather(d_ref, s):
        def issue(t, carry):
            for kk in range(TOP_K):
                pltpu.make_async_copy(ys_hbm.at[pl.ds(d_ref[0, 0, t * TOP_K + kk], 1)],
                                      buf_ref.at[s, kk, pl.ds(t, 1)], sems.at[s]).start()
            return carry

        lax.fori_loop(0, DISPATCH_TOKENS, issue, 0, unroll=ISSUE_UNROLL)

    @pl.when(i == 0)
    def _():
        gather(dest_ref, 0)

    @pl.when(i + 1 < pl.num_programs(0))
    def _():
        gather(next_ref, 1 - slot)

    for kk in range(TOP_K):
        pltpu.make_async_copy(ys_hbm.at[pl.ds(0, DISPATCH_TOKENS)], buf_ref.at[slot, kk], sems.at[slot]).wait()

    gates = gate_ref[...]
    x2 = x1_ref[...]
    for kk in range(TOP_K):
        x2 = x2 + gates[:, kk:kk + 1] * buf_ref[slot, kk]
    o_ref[...] = (x2 * lax.rsqrt(jnp.mean(x2 * x2, axis=-1, keepdims=True) + NORM_EPS)) * fw_ref[...]


def _combine(ys, dest, x1, gates, final_w):
    T, D = x1.shape
    n_blocks = T // DISPATCH_TOKENS
    dest3 = dest.reshape(n_blocks, 1, DISPATCH_TOKENS * TOP_K)
    row = lambda i: (i, 0)
    idx_blk = (1, 1, DISPATCH_TOKENS * TOP_K)
    return pl.pallas_call(
        _combine_body,
        grid=(n_blocks,),
        in_specs=[pl.BlockSpec(idx_blk, lambda i: (i, 0, 0), memory_space=pltpu.SMEM),
                  pl.BlockSpec(idx_blk, lambda i: (jnp.minimum(i + 1, n_blocks - 1), 0, 0),
                               memory_space=pltpu.SMEM),
                  pl.BlockSpec(memory_space=pl.ANY),
                  pl.BlockSpec((DISPATCH_TOKENS, D), row),
                  pl.BlockSpec((DISPATCH_TOKENS, TOP_K), row),
                  pl.BlockSpec((1, D), lambda i: (0, 0))],
        out_specs=pl.BlockSpec((DISPATCH_TOKENS, D), row),
        out_shape=jax.ShapeDtypeStruct((T, D), F32),
        scratch_shapes=[pltpu.VMEM((2, TOP_K, DISPATCH_TOKENS, D), F32), pltpu.SemaphoreType.DMA((2,))],
        compiler_params=_cparams("arbitrary"),
        name="moe_combine",
    )(dest3, dest3, ys, x1, gates, final_w.reshape(1, D))


def _moe(h, x1, idx, gates, rank, counts, mlp1_w, mlp1_b, mlp2_w, mlp2_b, final_w):
    T, D = h.shape
    n_exp = mlp1_w.shape[0]
    assert T % DISPATCH_TOKENS == 0
    n_padded = T * TOP_K + n_exp * MOE_ROWS
    n_blocks = n_padded // MOE_ROWS
    counts = counts.reshape(n_exp).astype(jnp.int32)
    padded = (counts + MOE_ROWS - 1) // MOE_ROWS * MOE_ROWS
    pend = jnp.cumsum(padded)
    pstart = pend - padded
    dest = (pstart[idx] + rank).astype(jnp.int32)
    starts = jnp.arange(n_blocks, dtype=jnp.int32) * MOE_ROWS
    blk_expert = jnp.minimum(jnp.sum((pend[None, :] <= starts[:, None]).astype(jnp.int32), axis=1),
                             n_exp - 1).astype(jnp.int32)
    n_used = (pend[-1:] // MOE_ROWS).astype(jnp.int32)

    w1 = _mlp1_regroup(mlp1_w)
    half = MXU_TILE // 2
    b1 = mlp1_b.reshape(n_exp, -1, half, 2).transpose(0, 1, 3, 2).reshape(n_exp, 1, -1)
    xs = _dispatch(h, dest, n_padded)
    ys = _experts(xs, blk_expert, n_used, w1, b1, mlp2_w.astype(BF16), mlp2_b[:, None, :])
    return _combine(ys, dest, x1, gates, final_w)


def kernel(x, positions, attn_norm_w, w_in, shift_mu, lambda_q1, lambda_k1, lambda_q2, lambda_k2, subln_w,
           rw_w0, rw_w2, rw_a0, rw_a2, rw_g2, rw_k_k, rw_k_a, rw_r_k, rw_lnx_w, rw_lnx_b, w_out, ffn_norm_w,
           router_w, router_b, mlp1_w, mlp1_b, mlp2_w, mlp2_b, final_norm_w):
    B, S, D = x.shape
    assert attn_norm_w.shape[0] == 1, "single-layer stack"
    l = LAYER_INDEX
    x2 = x.reshape(B * S, D)
    cos_t, sin_t = _rope_tables(positions)
    q, k, v, p_rw = _input_projection(x2, attn_norm_w[l], w_in[l], cos_t, sin_t)
    o_da = _diff_attention(q, k, v, lambda_q1[l], lambda_k1[l], lambda_q2[l], lambda_k2[l], subln_w[l], B, S)
    o_rw = _rwkv7(p_rw, shift_mu[l], rw_w0[l], rw_w2[l], rw_a0[l], rw_a2[l], rw_g2[l],
                  rw_k_k[l], rw_k_a[l], rw_r_k[l], rw_lnx_w[l], rw_lnx_b[l], B, S)
    x1, h, idx, gates, rank, counts = _outproj_router(o_da, o_rw, x2, w_out[l], ffn_norm_w[l],
                                                      router_w[l], router_b[l])
    out = _moe(h, x1, idx, gates, rank, counts, mlp1_w[l], mlp1_b[l], mlp2_w[l], mlp2_b[l], final_norm_w)
    return out.reshape(B, S, D)
```

```python
import functools
import math

import jax
import jax.numpy as jnp
from jax import lax
from jax.experimental import pallas as pl
from jax.experimental.pallas import tpu as pltpu

F32 = jnp.float32
BF16 = jnp.bfloat16

V7X_LANES = 128
V7X_VMEM_BYTES = 64 * 1024 * 1024
VMEM_LIMIT_BYTES = 56 * 1024 * 1024

DA_HEADS = 4
DA_HEAD_DIM = 64
DA_V_DIM = 2 * DA_HEAD_DIM
DA_WIDTH = DA_HEADS * DA_V_DIM
RW_HEAD = 64
DECAY_LORA = 64
AAA_LORA = 64
GATE_LORA = 128
ROPE_THETA = 10000.0
N_EXPERTS = 32
TOP_K = 4
SWIGLU_ALPHA = 1.702
SWIGLU_LIMIT = 7.0
NORM_EPS = 1e-5
GN_EPS = 64e-5
LAYER_INDEX = 0
LAM_INIT = 0.8 - 0.6 * math.exp(-0.3 * LAYER_INDEX)


def _cparams(*semantics):
    return pltpu.CompilerParams(dimension_semantics=semantics, vmem_limit_bytes=VMEM_LIMIT_BYTES)


def _pick_tile(n, want):
    t = min(n, want)
    while n % t:
        t //= 2
    return t


def _inproj_body(x_ref, nw_ref, w_ref, cos_ref, sin_ref, q_ref, k_ref, v_ref, prw_ref):
    x = x_ref[...]
    ms = jnp.mean(x * x, axis=-1, keepdims=True)
    h = (x * lax.rsqrt(ms + NORM_EPS)) * nw_ref[...]
    p = jnp.dot(h.astype(BF16), w_ref[...], preferred_element_type=F32)
    cos = cos_ref[...]
    sin = sin_ref[...]
    lane = lax.broadcasted_iota(jnp.int32, cos.shape, 1)
    first_half = (lane % DA_HEAD_DIM) < (DA_HEAD_DIM // 2)
    scale = DA_HEAD_DIM ** -0.5

    def rope(t):
        partner = jnp.where(first_half,
                            pltpu.roll(t, V7X_LANES - DA_HEAD_DIM // 2, 1),
                            pltpu.roll(t, DA_HEAD_DIM // 2, 1))
        return t * cos + partner * sin

    for hh in range(DA_HEADS):
        sl = slice(hh * DA_V_DIM, (hh + 1) * DA_V_DIM)
        q_ref[:, sl] = (rope(p[:, sl]) * scale).astype(q_ref.dtype)
        ksl = slice(DA_WIDTH + hh * DA_V_DIM, DA_WIDTH + (hh + 1) * DA_V_DIM)
        k_ref[:, sl] = rope(p[:, ksl]).astype(k_ref.dtype)
    v_ref[...] = p[:, 2 * DA_WIDTH:3 * DA_WIDTH].astype(v_ref.dtype)
    prw_ref[...] = p[:, 3 * DA_WIDTH:]


def _input_projection(x2, norm_w, w_in, cos_t, sin_t):
    T, D = x2.shape
    n_cols = w_in.shape[1]
    rw_cols = n_cols - 3 * DA_WIDTH
    tm = _pick_tile(T, 512)
    row = lambda i: (i, 0)
    fixed = lambda i: (0, 0)
    return pl.pallas_call(
        _inproj_body,
        grid=(T // tm,),
        in_specs=[pl.BlockSpec((tm, D), row),
                  pl.BlockSpec((1, D), fixed),
                  pl.BlockSpec((D, n_cols), fixed),
                  pl.BlockSpec((tm, DA_V_DIM), row),
                  pl.BlockSpec((tm, DA_V_DIM), row)],
        out_specs=[pl.BlockSpec((tm, DA_WIDTH), row),
                   pl.BlockSpec((tm, DA_WIDTH), row),
                   pl.BlockSpec((tm, DA_WIDTH), row),
                   pl.BlockSpec((tm, rw_cols), row)],
        out_shape=[jax.ShapeDtypeStruct((T, DA_WIDTH), BF16),
                   jax.ShapeDtypeStruct((T, DA_WIDTH), BF16),
                   jax.ShapeDtypeStruct((T, DA_WIDTH), BF16),
                   jax.ShapeDtypeStruct((T, rw_cols), F32)],
        compiler_params=_cparams("parallel"),
        name="input_projection",
    )(x2, norm_w.reshape(1, D), w_in.astype(BF16), cos_t, sin_t)


def _rope_tables(positions):
    half = DA_HEAD_DIM // 2
    inv = ROPE_THETA ** (-jnp.arange(0, DA_HEAD_DIM, 2, dtype=F32) / DA_HEAD_DIM)
    ang = positions.reshape(-1).astype(F32)[:, None] * inv
    cos, sin = jnp.cos(ang), jnp.sin(ang)
    cos_t = jnp.tile(cos, (1, DA_V_DIM // half))
    sin_t = jnp.tile(jnp.concatenate([-sin, sin], axis=-1), (1, DA_V_DIM // DA_HEAD_DIM))
    return cos_t, sin_t


ATTN_Q_SUB = 128


def _diff_attn_body(q_ref, k_ref, v_ref, lq1_ref, lk1_ref, lq2_ref, lk2_ref, sw_ref, o_ref,
                    m_ref, acc_ref, *, tq, tk, q_sub):
    qi = pl.program_id(2)
    q = q_ref[0]
    lane = lax.broadcasted_iota(jnp.int32, q.shape, 1)
    zero = jnp.zeros_like(q)
    q_heads = (jnp.where(lane < DA_HEAD_DIM, q, zero), jnp.where(lane >= DA_HEAD_DIM, q, zero))
    n_lane_tiles = tk // V7X_LANES

    m_ref[...] = jnp.full(m_ref.shape, -jnp.inf, F32)
    acc_ref[...] = jnp.zeros(acc_ref.shape, F32)

    def step(j, masked):
        ks = pl.multiple_of(j * tk, tk)
        kb = k_ref[0, pl.ds(ks, tk), :]
        vb = v_ref[0, pl.ds(ks, tk), :]
        v_ext = jnp.concatenate([vb, jnp.ones_like(vb)], axis=1)
        for r0 in range(0, tq, q_sub):
            rows = slice(r0, r0 + q_sub)
            for c in range(2):
                s = lax.dot_general(q_heads[c][rows], kb, (((1,), (1,)), ((), ())),
                                    preferred_element_type=F32)
                if masked:
                    ri = lax.broadcasted_iota(jnp.int32, s.shape, 0) + r0
                    ci = lax.broadcasted_iota(jnp.int32, s.shape, 1)
                    s = jnp.where(ci <= ri, s, -jnp.inf)
                m_prev = m_ref[c, rows, :]
                m_new = jnp.maximum(m_prev, jnp.max(s, axis=-1, keepdims=True))
                alpha = jnp.exp(m_prev - m_new)
                p = jnp.concatenate(
                    [jnp.exp(s[:, t * V7X_LANES:(t + 1) * V7X_LANES] - m_new).astype(vb.dtype)
                     for t in range(n_lane_tiles)], axis=1)
                pv = jnp.dot(p, v_ext, preferred_element_type=F32)
                acc_ref[c, rows, :] = jnp.concatenate([alpha, alpha], axis=1) * acc_ref[c, rows, :] + pv
                m_ref[c, rows, :] = m_new

    def full_step(j, carry):
        step(j, False)
        return carry

    lax.fori_loop(0, qi * (tq // tk), full_step, 0)
    for d in range(tq // tk):
        step(qi * (tq // tk) + d, True)

    lam = (jnp.exp(jnp.sum(lq1_ref[...] * lk1_ref[...], axis=-1, keepdims=True))
           - jnp.exp(jnp.sum(lq2_ref[...] * lk2_ref[...], axis=-1, keepdims=True)) + LAM_INIT)
    a0, a1 = acc_ref[0], acc_ref[1]
    o = a0[:, :DA_V_DIM] / a0[:, DA_V_DIM:] - lam * (a1[:, :DA_V_DIM] / a1[:, DA_V_DIM:])
    o = o * lax.rsqrt(jnp.mean(o * o, axis=-1, keepdims=True) + NORM_EPS)
    o_ref[0] = (o * sw_ref[...] * (1.0 - LAM_INIT)).astype(o_ref.dtype)


def _diff_attention(q, k, v, lq1, lk1, lq2, lk2, subln_w, B, S):
    tq = tk = _pick_tile(S, 512)
    q_sub = _pick_tile(tq, ATTN_Q_SUB)
    q3, k3, v3 = (t.reshape(B, S, DA_WIDTH) for t in (q, k, v))
    vec = lambda a: a.reshape(1, -1).astype(F32)
    blk = lambda b, h, i: (b, i, h)
    seq = lambda b, h, i: (b, 0, h)
    fixed = lambda b, h, i: (0, 0)
    out = pl.pallas_call(
        functools.partial(_diff_attn_body, tq=tq, tk=tk, q_sub=q_sub),
        grid=(B, DA_HEADS, S // tq),
        in_specs=[pl.BlockSpec((1, tq, DA_V_DIM), blk),
                  pl.BlockSpec((1, S, DA_V_DIM), seq),
                  pl.BlockSpec((1, S, DA_V_DIM), seq),
                  pl.BlockSpec((1, DA_HEAD_DIM), fixed),
                  pl.BlockSpec((1, DA_HEAD_DIM), fixed),
                  pl.BlockSpec((1, DA_HEAD_DIM), fixed),
                  pl.BlockSpec((1, DA_HEAD_DIM), fixed),
                  pl.BlockSpec((1, DA_V_DIM), fixed)],
        out_specs=pl.BlockSpec((1, tq, DA_V_DIM), blk),
        out_shape=jax.ShapeDtypeStruct((B, S, DA_WIDTH), BF16),
        scratch_shapes=[pltpu.VMEM((2, tq, V7X_LANES), F32),
                        pltpu.VMEM((2, tq, 2 * DA_V_DIM), F32)],
        compiler_params=_cparams("parallel", "parallel", "arbitrary"),
        name="diff_attention",
    )(q3, k3, v3, vec(lq1), vec(lk1), vec(lq2), vec(lk2), vec(subln_w))
    return out.reshape(B * S, DA_WIDTH)


RW_CHUNK = 64
RW_PAIR = 2 * RW_HEAD
RW_INV_DOUBLINGS = 5


def _bdot(a, b):
    return jnp.dot(a.astype(BF16), b.astype(BF16), preferred_element_type=F32)


def _bdot_nt(a, b):
    return lax.dot_general(a.astype(BF16), b.astype(BF16), (((1,), (1,)), ((), ())),
                           preferred_element_type=F32)


def _bdot_tn(a, b):
    return lax.dot_general(a.astype(BF16), b.astype(BF16), (((0,), (0,)), ((), ())),
                           preferred_element_type=F32)


def _split3_dot(m, x):
    mb = m.astype(BF16)
    x0 = x.astype(BF16)
    r1 = x - x0.astype(F32)
    x1 = r1.astype(BF16)
    x2 = (r1 - x1.astype(F32)).astype(BF16)
    dot = lambda t: jnp.dot(mb, t, preferred_element_type=F32)
    return dot(x0) + dot(x1) + dot(x2)


def _rwkv_body(p_ref, prev_ref, mu_ref, w0_ref, a0_ref, kk_ref, ka_ref, rk_ref, lnw_ref, lnb_ref,
               wa_ref, g2_ref, o_ref,
               s_ref, rt_ref, kh_ref, at_ref, bh_ref, v_ref, gl_ref, y_ref, *, tb, width):
    i = pl.program_id(1)
    n_pairs = width // RW_PAIR
    C = RW_CHUNK

    @pl.when(i == 0)
    def _():
        s_ref[...] = jnp.zeros(s_ref.shape, F32)

    p = p_ref[0]
    row = lax.broadcasted_iota(jnp.int32, p.shape, 0)
    last_prev = jnp.where(i == 0, 0.0, prev_ref[0, 7:8, :])
    prev = jnp.where(row == 0, last_prev, pltpu.roll(p, 1, 0))
    ps = p + mu_ref[...] * (prev - p)
    r = ps[:, 0:width]
    k = ps[:, width:2 * width]
    v = ps[:, 2 * width:3 * width]
    c3 = 3 * width
    wa_in = ps[:, c3:c3 + DECAY_LORA + AAA_LORA]
    lane = lax.broadcasted_iota(jnp.int32, wa_in.shape, 1)
    wa_in = jnp.where(lane < DECAY_LORA, jnp.tanh(wa_in), wa_in)
    wa = _bdot(wa_in, wa_ref[...])
    gd = ps[:, c3 + DECAY_LORA + AAA_LORA:]
    g = _bdot(jax.nn.sigmoid(gd), g2_ref[...])

    wl = w0_ref[...] + wa[:, :width]
    softplus_neg = jnp.maximum(-wl, 0.0) + jnp.log1p(jnp.exp(-jnp.abs(wl)))
    logw = -jnp.exp(-softplus_neg - 0.5)
    a = jax.nn.sigmoid(a0_ref[...] + wa[:, width:])

    hr = lax.broadcasted_iota(jnp.int32, (width, width), 0) // RW_HEAD
    hc = lax.broadcasted_iota(jnp.int32, (width, width), 1) // RW_HEAD
    head_ones = jnp.where(hr == hc, 1.0, 0.0).astype(BF16)
    head_sum = lambda t: jnp.dot(t.astype(BF16), head_ones, preferred_element_type=F32)

    kk = k * kk_ref[...]
    kk = kk / jnp.maximum(jnp.sqrt(head_sum(kk * kk)), 1e-12)
    k2 = k * (1.0 + (a - 1.0) * ka_ref[...])
    bonus = head_sum(r * k2 * rk_ref[...]) * v

    tr = lax.broadcasted_iota(jnp.int32, (tb, tb), 0)
    tc = lax.broadcasted_iota(jnp.int32, (tb, tb), 1)
    chunk_tril = jnp.where((tr // C == tc // C) & (tc <= tr), 1.0, 0.0)
    cum = _split3_dot(chunk_tril, logw)
    g_inc = jnp.exp(cum)
    g_inv = jnp.exp(-cum)
    g_exc = jnp.exp(cum - logw)
    rt_ref[...] = r * g_inc
    kh_ref[...] = k2 * g_inv
    at_ref[...] = -kk * g_exc
    bh_ref[...] = kk * a * g_inv
    v_ref[...] = v
    gl_ref[...] = g_inc

    lane_p = lax.broadcasted_iota(jnp.int32, (C, RW_PAIR), 1)
    first = lane_p < RW_HEAD
    ri = lax.broadcasted_iota(jnp.int32, (RW_PAIR, RW_PAIR), 0)
    ci = lax.broadcasted_iota(jnp.int32, (RW_PAIR, RW_PAIR), 1)
    strict = ci < ri
    incl = ci <= ri
    eye = jnp.where(ci == ri, 1.0, 0.0)

    def stack(t):
        return jnp.concatenate([jnp.where(first, t, 0.0), jnp.where(first, 0.0, t)], axis=0)

    def chunk_step(c, carry):
        c0 = pl.multiple_of(c * C, C)
        rows = pl.ds(c0, C)
        for j in range(n_pairs):
            sl = slice(j * RW_PAIR, (j + 1) * RW_PAIR)
            rts, khs = stack(rt_ref[rows, sl]), stack(kh_ref[rows, sl])
            ats, bhs = stack(at_ref[rows, sl]), stack(bh_ref[rows, sl])
            vs = stack(v_ref[rows, sl])
            g_last = gl_ref[pl.ds(pl.multiple_of(c0 + C - 8, 8), 8), sl][7:8]

            sc = _bdot_nt(jnp.concatenate([ats, rts], axis=0), jnp.concatenate([bhs, khs], axis=0))
            l_ab = jnp.where(strict, sc[:RW_PAIR, :RW_PAIR], 0.0)
            l_ak = jnp.where(strict, sc[:RW_PAIR, RW_PAIR:], 0.0)
            a_rb = jnp.where(incl, sc[RW_PAIR:, :RW_PAIR], 0.0)
            a_rk = jnp.where(incl, sc[RW_PAIR:, RW_PAIR:], 0.0)

            pw = l_ab
            tinv = eye + l_ab
            for _ in range(RW_INV_DOUBLINGS):
                pw = _bdot(pw, pw)
                tinv = tinv + _bdot(tinv, pw)

            lv = _bdot(l_ak, vs)
            taw = _bdot(tinv, jnp.concatenate([ats, lv], axis=1))
            ta, w_u = taw[:, :RW_PAIR], taw[:, RW_PAIR:]
            ar = _bdot(jnp.concatenate([a_rb, a_rk], axis=1),
                       jnp.concatenate([taw, jnp.concatenate([jnp.zeros_like(vs), vs], axis=1)], axis=0))
            qt = rts + ar[:, :RW_PAIR]
            y0 = ar[:, RW_PAIR:]
            g0 = _bdot_tn(vs, khs)

            s0 = s_ref[j]
            m1 = _bdot_nt(jnp.concatenate([qt, ta], axis=0), s0)
            y = m1[:RW_PAIR] + y0
            u = m1[RW_PAIR:] + w_u
            s_ref[j] = (s0 + g0 + _bdot_tn(u, bhs)) * g_last
            y_ref[rows, sl] = y[:C] + y[C:]
        return carry

    lax.fori_loop(0, tb // C, chunk_step, 0)

    y = y_ref[...]
    inv_n = 1.0 / RW_HEAD
    mean = head_sum(y) * inv_n
    yc = y - mean
    var = head_sum(yc * yc) * inv_n
    yn = yc * lax.rsqrt(var + GN_EPS) * lnw_ref[...] + lnb_ref[...]
    o_ref[0] = ((yn + bonus) * g).astype(o_ref.dtype)


def _rwkv7(p_rw, shift_mu, w0, w2, a0, a2, g2, k_k, k_a, r_k, lnx_w, lnx_b, B, S):
    cols = p_rw.shape[-1]
    width = (cols - DECAY_LORA - AAA_LORA - GATE_LORA) // 3
    tb = _pick_tile(S, 256)
    assert tb % RW_CHUNK == 0 and width % RW_PAIR == 0
    p3 = p_rw.reshape(B, S, cols)
    vec = lambda t: t.reshape(1, -1).astype(F32)
    zeros = jnp.zeros_like(w2)
    wa = jnp.concatenate([jnp.concatenate([w2, zeros], axis=1),
                          jnp.concatenate([jnp.zeros_like(a2), a2], axis=1)], axis=0).astype(BF16)
    blk = lambda b, i: (b, i, 0)
    prev_blk = lambda b, i: (b, jnp.maximum(i * (tb // 8) - 1, 0), 0)
    fixed = lambda b, i: (0, 0)
    wide = pltpu.VMEM((tb, width), F32)
    out = pl.pallas_call(
        functools.partial(_rwkv_body, tb=tb, width=width),
        grid=(B, S // tb),
        in_specs=[pl.BlockSpec((1, tb, cols), blk),
                  pl.BlockSpec((1, 8, cols), prev_blk),
                  pl.BlockSpec((1, cols), fixed)]
                 + [pl.BlockSpec((1, width), fixed)] * 7
                 + [pl.BlockSpec((DECAY_LORA + AAA_LORA, 2 * width), fixed),
                    pl.BlockSpec((GATE_LORA, width), fixed)],
        out_specs=pl.BlockSpec((1, tb, width), blk),
        out_shape=jax.ShapeDtypeStruct((B, S, width), BF16),
        scratch_shapes=[pltpu.VMEM((width // RW_PAIR, RW_PAIR, RW_PAIR), F32)] + [wide] * 7,
        compiler_params=_cparams("parallel", "arbitrary"),
        name="rwkv7_mix",
    )(p3, p3, vec(shift_mu), vec(w0), vec(a0), vec(k_k), vec(k_a), vec(r_k), vec(lnx_w), vec(lnx_b),
      wa, g2.astype(BF16))
    return out.reshape(B * S, width)


def _outproj_router_body(oda_ref, orw_ref, x_ref, wa_ref, wb_ref, nw_ref, rw_ref, rb_ref,
                         x1_ref, h_ref, idx_ref, gate_ref, rank_ref, cnt_ref, carry_ref):
    i = pl.program_id(0)

    @pl.when(i == 0)
    def _():
        carry_ref[...] = jnp.zeros(carry_ref.shape, F32)

    x1 = (x_ref[...] + jnp.dot(oda_ref[...], wa_ref[...], preferred_element_type=F32)
          + jnp.dot(orw_ref[...], wb_ref[...], preferred_element_type=F32))
    x1_ref[...] = x1
    h = (x1 * lax.rsqrt(jnp.mean(x1 * x1, axis=-1, keepdims=True) + NORM_EPS)) * nw_ref[...]
    h_ref[...] = h
    logits = jnp.dot(h, rw_ref[...], preferred_element_type=F32,
                     precision=lax.Precision.HIGHEST) + rb_ref[...]
    tm, n_exp = logits.shape
    lane = lax.broadcasted_iota(jnp.int32, logits.shape, 1).astype(F32)
    out_lane = lax.broadcasted_iota(jnp.int32, (tm, TOP_K), 1)

    vals, idxs = [], []
    rest = logits
    for _ in range(TOP_K):
        m = jnp.max(rest, axis=-1, keepdims=True)
        sel = jnp.min(jnp.where(rest == m, lane, float(n_exp)), axis=-1, keepdims=True)
        vals.append(m)
        idxs.append(sel)
        rest = jnp.where(lane == sel, -jnp.inf, rest)
    exps = [jnp.exp(v - vals[0]) for v in vals]
    denom = exps[0]
    for e in exps[1:]:
        denom = denom + e

    hot = [jnp.where(lane == s, 1.0, 0.0) for s in idxs]
    hot_all = hot[0]
    for t in hot[1:]:
        hot_all = hot_all + t
    tr = lax.broadcasted_iota(jnp.int32, (tm, tm), 0)
    tc = lax.broadcasted_iota(jnp.int32, (tm, tm), 1)
    before = jnp.dot(jnp.where(tc < tr, 1.0, 0.0).astype(BF16), hot_all.astype(BF16),
                     preferred_element_type=F32) + carry_ref[...]

    idx_out = jnp.zeros((tm, TOP_K), F32)
    gate_out = jnp.zeros((tm, TOP_K), F32)
    rank_out = jnp.zeros((tm, TOP_K), F32)
    for kk in range(TOP_K):
        rank_k = jnp.sum(hot[kk] * before, axis=-1, keepdims=True)
        idx_out = jnp.where(out_lane == kk, idxs[kk], idx_out)
        gate_out = jnp.where(out_lane == kk, exps[kk] / denom, gate_out)
        rank_out = jnp.where(out_lane == kk, rank_k, rank_out)
    idx_ref[...] = idx_out.astype(jnp.int32)
    gate_ref[...] = gate_out
    rank_ref[...] = rank_out.astype(jnp.int32)
    carry_ref[...] += jnp.sum(hot_all, axis=0, keepdims=True)
    cnt_ref[...] = carry_ref[...]


def _outproj_router(o_da, o_rw, x2, w_out, ffn_norm_w, router_w, router_b):
    T, D = x2.shape
    n_exp = router_w.shape[1]
    tm = _pick_tile(T, 512)
    wa = w_out[:DA_WIDTH].astype(BF16)
    wb = w_out[DA_WIDTH:].astype(BF16)
    row = lambda i: (i, 0)
    fixed = lambda i: (0, 0)
    return pl.pallas_call(
        _outproj_router_body,
        grid=(T // tm,),
        in_specs=[pl.BlockSpec((tm, DA_WIDTH), row),
                  pl.BlockSpec((tm, o_rw.shape[1]), row),
                  pl.BlockSpec((tm, D), row),
                  pl.BlockSpec(wa.shape, fixed),
                  pl.BlockSpec(wb.shape, fixed),
                  pl.BlockSpec((1, D), fixed),
                  pl.BlockSpec((D, n_exp), fixed),
                  pl.BlockSpec((1, n_exp), fixed)],
        out_specs=[pl.BlockSpec((tm, D), row),
                   pl.BlockSpec((tm, D), row),
                   pl.BlockSpec((tm, TOP_K), row),
                   pl.BlockSpec((tm, TOP_K), row),
                   pl.BlockSpec((tm, TOP_K), row),
                   pl.BlockSpec((1, n_exp), fixed)],
        out_shape=[jax.ShapeDtypeStruct((T, D), F32),
                   jax.ShapeDtypeStruct((T, D), F32),
                   jax.ShapeDtypeStruct((T, TOP_K), jnp.int32),
                   jax.ShapeDtypeStruct((T, TOP_K), F32),
                   jax.ShapeDtypeStruct((T, TOP_K), jnp.int32),
                   jax.ShapeDtypeStruct((1, n_exp), F32)],
        scratch_shapes=[pltpu.VMEM((1, n_exp), F32)],
        compiler_params=_cparams("arbitrary"),
        name="outproj_router",
    )(o_da, o_rw, x2, wa, wb, ffn_norm_w.reshape(1, D), router_w, router_b.reshape(1, n_exp))


MOE_ROWS = 256
DISPATCH_TOKENS = 256
ISSUE_UNROLL = 8
MXU_TILE = 256


def _dispatch_body(dest_ref, h_ref, init_hbm, xs_hbm, sem):
    del init_hbm

    def issue(t, carry):
        src = h_ref.at[pl.ds(t, 1)]
        for kk in range(TOP_K):
            pltpu.make_async_copy(src, xs_hbm.at[pl.ds(dest_ref[0, 0, t * TOP_K + kk], 1)], sem).start()
        return carry

    lax.fori_loop(0, DISPATCH_TOKENS, issue, 0, unroll=ISSUE_UNROLL)
    for _ in range(TOP_K):
        pltpu.make_async_copy(h_ref, xs_hbm.at[pl.ds(0, DISPATCH_TOKENS)], sem).wait()


def _dispatch(h, dest, n_padded):
    T, D = h.shape
    n_blocks = T // DISPATCH_TOKENS
    dest3 = dest.reshape(n_blocks, 1, DISPATCH_TOKENS * TOP_K)
    return pl.pallas_call(
        _dispatch_body,
        grid=(n_blocks,),
        in_specs=[pl.BlockSpec((1, 1, DISPATCH_TOKENS * TOP_K), lambda i: (i, 0, 0), memory_space=pltpu.SMEM),
                  pl.BlockSpec((DISPATCH_TOKENS, D), lambda i: (i, 0)),
                  pl.BlockSpec(memory_space=pl.ANY)],
        out_specs=pl.BlockSpec(memory_space=pl.ANY),
        out_shape=jax.ShapeDtypeStruct((n_padded, D), h.dtype),
        scratch_shapes=[pltpu.SemaphoreType.DMA(())],
        input_output_aliases={2: 0},
        compiler_params=_cparams("arbitrary"),
        name="moe_dispatch",
    )(dest3, h, jnp.zeros((n_padded, D), h.dtype))


def _mlp1_regroup_body(w_ref, o_ref):
    half = MXU_TILE // 2
    r = lax.broadcasted_iota(jnp.int32, (MXU_TILE, MXU_TILE), 0)
    c = lax.broadcasted_iota(jnp.int32, (MXU_TILE, MXU_TILE), 1)
    perm = jnp.where(c == (r >> 1) + (r & 1) * half, 1.0, 0.0).astype(BF16)
    for g in range(w_ref.shape[2] // MXU_TILE):
        sl = slice(g * MXU_TILE, (g + 1) * MXU_TILE)
        o_ref[0, :, sl] = jnp.dot(w_ref[0, :, sl].astype(BF16), perm,
                                  preferred_element_type=F32).astype(o_ref.dtype)


def _mlp1_regroup(mlp1_w):
    n_exp, D, cols = mlp1_w.shape
    tc = _pick_tile(cols, 2 * MXU_TILE)
    blk = lambda e, j: (e, 0, j)
    return pl.pallas_call(
        _mlp1_regroup_body,
        grid=(n_exp, cols // tc),
        in_specs=[pl.BlockSpec((1, D, tc), blk)],
        out_specs=pl.BlockSpec((1, D, tc), blk),
        out_shape=jax.ShapeDtypeStruct(mlp1_w.shape, BF16),
        compiler_params=_cparams("parallel", "parallel"),
        name="mlp1_regroup",
    )(mlp1_w)


def _experts_body(be_ref, nu_ref, xs_ref, w1_ref, b1_ref, w2_ref, b2_ref, y_ref):
    del be_ref
    i = pl.program_id(0)
    half = MXU_TILE // 2

    @pl.when(i < nu_ref[0])
    def _():
        x = xs_ref[...].astype(BF16)
        gu = jnp.dot(x, w1_ref[0], preferred_element_type=F32) + b1_ref[0]
        acts = []
        for g in range(gu.shape[1] // MXU_TILE):
            gate = jnp.minimum(gu[:, g * MXU_TILE:g * MXU_TILE + half], SWIGLU_LIMIT)
            up = jnp.clip(gu[:, g * MXU_TILE + half:(g + 1) * MXU_TILE], -SWIGLU_LIMIT, SWIGLU_LIMIT)
            acts.append(((up + 1.0) * gate * jax.nn.sigmoid(SWIGLU_ALPHA * gate)).astype(BF16))
        act = jnp.concatenate(acts, axis=1)
        y_ref[...] = jnp.dot(act, w2_ref[0], preferred_element_type=F32) + b2_ref[0]

    @pl.when(i >= nu_ref[0])
    def _():
        y_ref[...] = jnp.zeros(y_ref.shape, y_ref.dtype)


def _experts(xs, blk_expert, n_used, w1, b1, w2, b2):
    n_padded, D = xs.shape
    n_blocks = n_padded // MOE_ROWS
    gu_cols = w1.shape[2]
    xrow = lambda i, be, nu: (jnp.maximum(jnp.minimum(i, nu[0] - 1), 0), 0)
    yrow = lambda i, be, nu: (i, 0)
    wsel = lambda i, be, nu: (be[i], 0, 0)
    return pl.pallas_call(
        _experts_body,
        grid_spec=pltpu.PrefetchScalarGridSpec(
            num_scalar_prefetch=2,
            grid=(n_blocks,),
            in_specs=[pl.BlockSpec((MOE_ROWS, D), xrow),
                      pl.BlockSpec((1, D, gu_cols), wsel),
                      pl.BlockSpec((1, 1, gu_cols), wsel),
                      pl.BlockSpec((1, gu_cols // 2, D), wsel),
                      pl.BlockSpec((1, 1, D), wsel)],
            out_specs=pl.BlockSpec((MOE_ROWS, D), yrow)),
        out_shape=jax.ShapeDtypeStruct((n_padded, D), F32),
        compiler_params=_cparams("arbitrary"),
        name="moe_experts",
    )(blk_expert, n_used, xs, w1, b1, w2, b2)


def _combine_body(dest_ref, next_ref, ys_hbm, x1_ref, gate_ref, fw_ref, o_ref, buf_ref, sems):
    i = pl.program_id(0)
    slot = i % 2

    def gather(d_ref, s):
        def issue(t, carry):
            for kk in range(TOP_K):
                pltpu.make_async_copy(ys_hbm.at[pl.ds(d_ref[0, 0, t * TOP_K + kk], 1)],
                                      buf_ref.at[s, kk, pl.ds(t, 1)], sems.at[s]).start()
            return carry

        lax.fori_loop(0, DISPATCH_TOKENS, issue, 0, unroll=ISSUE_UNROLL)

    @pl.when(i == 0)
    def _():
        gather(dest_ref, 0)

    @pl.when(i + 1 < pl.num_programs(0))
    def _():
        gather(next_ref, 1 - slot)

    for kk in range(TOP_K):
        pltpu.make_async_copy(ys_hbm.at[pl.ds(0, DISPATCH_TOKENS)], buf_ref.at[slot, kk], sems.at[slot]).wait()

    gates = gate_ref[...]
    x2 = x1_ref[...]
    for kk in range(TOP_K):
        x2 = x2 + gates[:, kk:kk + 1] * buf_ref[slot, kk]
    o_ref[...] = (x2 * lax.rsqrt(jnp.mean(x2 * x2, axis=-1, keepdims=True) + NORM_EPS)) * fw_ref[...]


def _combine(ys, dest, x1, gates, final_w):
    T, D = x1.shape
    n_blocks = T // DISPATCH_TOKENS
    dest3 = dest.reshape(n_blocks, 1, DISPATCH_TOKENS * TOP_K)
    row = lambda i: (i, 0)
    idx_blk = (1, 1, DISPATCH_TOKENS * TOP_K)
    return pl.pallas_call(
        _combine_body,
        grid=(n_blocks,),
        in_specs=[pl.BlockSpec(idx_blk, lambda i: (i, 0, 0), memory_space=pltpu.SMEM),
                  pl.BlockSpec(idx_blk, lambda i: (jnp.minimum(i + 1, n_blocks - 1), 0, 0),
                               memory_space=pltpu.SMEM),
                  pl.BlockSpec(memory_space=pl.ANY),
                  pl.BlockSpec((DISPATCH_TOKENS, D), row),
                  pl.BlockSpec((DISPATCH_TOKENS, TOP_K), row),
                  pl.BlockSpec((1, D), lambda i: (0, 0))],
        out_specs=pl.BlockSpec((DISPATCH_TOKENS, D), row),
        out_shape=jax.ShapeDtypeStruct((T, D), F32),
        scratch_shapes=[pltpu.VMEM((2, TOP_K, DISPATCH_TOKENS, D), F32), pltpu.SemaphoreType.DMA((2,))],
        compiler_params=_cparams("arbitrary"),
        name="moe_combine",
    )(dest3, dest3, ys, x1, gates, final_w.reshape(1, D))


def _moe(h, x1, idx, gates, rank, counts, mlp1_w, mlp1_b, mlp2_w, mlp2_b, final_w):
    T, D = h.shape
    n_exp = mlp1_w.shape[0]
    assert T % DISPATCH_TOKENS == 0
    n_padded = T * TOP_K + n_exp * MOE_ROWS
    n_blocks = n_padded // MOE_ROWS
    counts = counts.reshape(n_exp).astype(jnp.int32)
    padded = (counts + MOE_ROWS - 1) // MOE_ROWS * MOE_ROWS
    pend = jnp.cumsum(padded)
    pstart = pend - padded
    dest = (pstart[idx] + rank).astype(jnp.int32)
    starts = jnp.arange(n_blocks, dtype=jnp.int32) * MOE_ROWS
    blk_expert = jnp.minimum(jnp.sum((pend[None, :] <= starts[:, None]).astype(jnp.int32), axis=1),
                             n_exp - 1).astype(jnp.int32)
    n_used = (pend[-1:] // MOE_ROWS).astype(jnp.int32)

    w1 = _mlp1_regroup(mlp1_w)
    half = MXU_TILE // 2
    b1 = mlp1_b.reshape(n_exp, -1, half, 2).transpose(0, 1, 3, 2).reshape(n_exp, 1, -1)
    xs = _dispatch(h, dest, n_padded)
    ys = _experts(xs, blk_expert, n_used, w1, b1, mlp2_w.astype(BF16), mlp2_b[:, None, :])
    return _combine(ys, dest, x1, gates, final_w)


def kernel(x, positions, attn_norm_w, w_in, shift_mu, lambda_q1, lambda_k1, lambda_q2, lambda_k2, subln_w,
           rw_w0, rw_w2, rw_a0, rw_a2, rw_g2, rw_k_k, rw_k_a, rw_r_k, rw_lnx_w, rw_lnx_b, w_out, ffn_norm_w,
           router_w, router_b, mlp1_w, mlp1_b, mlp2_w, mlp2_b, final_norm_w):
    B, S, D = x.shape
    assert attn_norm_w.shape[0] == 1, "single-layer stack"
    l = LAYER_INDEX
    x2 = x.reshape(B * S, D)
    cos_t, sin_t = _rope_tables(positions)
    q, k, v, p_rw = _input_projection(x2, attn_norm_w[l], w_in[l], cos_t, sin_t)
    o_da = _diff_attention(q, k, v, lambda_q1[l], lambda_k1[l], lambda_q2[l], lambda_k2[l], subln_w[l], B, S)
    o_rw = _rwkv7(p_rw, shift_mu[l], rw_w0[l], rw_w2[l], rw_a0[l], rw_a2[l], rw_g2[l],
                  rw_k_k[l], rw_k_a[l], rw_r_k[l], rw_lnx_w[l], rw_lnx_b[l], B, S)
    x1, h, idx, gates, rank, counts = _outproj_router(o_da, o_rw, x2, w_out[l], ffn_norm_w[l],
                                                      router_w[l], router_b[l])
    out = _moe(h, x1, idx, gates, rank, counts, mlp1_w[l], mlp1_b[l], mlp2_w[l], mlp2_b[l], final_norm_w)
    return out.reshape(B, S, D)
```

```python
import functools
import math

import jax
import jax.numpy as jnp
from jax import lax
from jax.experimental import pallas as pl
from jax.experimental.pallas import tpu as pltpu

F32 = jnp.float32
BF16 = jnp.bfloat16

V7X_LANES = 128
V7X_VMEM_BYTES = 64 * 1024 * 1024
VMEM_LIMIT_BYTES = 56 * 1024 * 1024

DA_HEADS = 4
DA_HEAD_DIM = 64
DA_V_DIM = 2 * DA_HEAD_DIM
DA_WIDTH = DA_HEADS * DA_V_DIM
RW_HEAD = 64
DECAY_LORA = 64
AAA_LORA = 64
GATE_LORA = 128
ROPE_THETA = 10000.0
N_EXPERTS = 32
TOP_K = 4
SWIGLU_ALPHA = 1.702
SWIGLU_LIMIT = 7.0
NORM_EPS = 1e-5
GN_EPS = 64e-5
LAYER_INDEX = 0
LAM_INIT = 0.8 - 0.6 * math.exp(-0.3 * LAYER_INDEX)


def _cparams(*semantics):
    return pltpu.CompilerParams(dimension_semantics=semantics, vmem_limit_bytes=VMEM_LIMIT_BYTES)


def _pick_tile(n, want):
    t = min(n, want)
    while n % t:
        t //= 2
    return t


def _inproj_body(x_ref, nw_ref, w_ref, cos_ref, sin_ref, q_ref, k_ref, v_ref, prw_ref):
    x = x_ref[...]
    ms = jnp.mean(x * x, axis=-1, keepdims=True)
    h = (x * lax.rsqrt(ms + NORM_EPS)) * nw_ref[...]
    p = jnp.dot(h.astype(BF16), w_ref[...], preferred_element_type=F32)
    cos = cos_ref[...]
    sin = sin_ref[...]
    lane = lax.broadcasted_iota(jnp.int32, cos.shape, 1)
    first_half = (lane % DA_HEAD_DIM) < (DA_HEAD_DIM // 2)
    scale = DA_HEAD_DIM ** -0.5

    def rope(t):
        partner = jnp.where(first_half,
                            pltpu.roll(t, V7X_LANES - DA_HEAD_DIM // 2, 1),
                            pltpu.roll(t, DA_HEAD_DIM // 2, 1))
        return t * cos + partner * sin

    for hh in range(DA_HEADS):
        sl = slice(hh * DA_V_DIM, (hh + 1) * DA_V_DIM)
        q_ref[:, sl] = (rope(p[:, sl]) * scale).astype(q_ref.dtype)
        ksl = slice(DA_WIDTH + hh * DA_V_DIM, DA_WIDTH + (hh + 1) * DA_V_DIM)
        k_ref[:, sl] = rope(p[:, ksl]).astype(k_ref.dtype)
    v_ref[...] = p[:, 2 * DA_WIDTH:3 * DA_WIDTH].astype(v_ref.dtype)
    prw_ref[...] = p[:, 3 * DA_WIDTH:]


def _input_projection(x2, norm_w, w_in, cos_t, sin_t):
    T, D = x2.shape
    n_cols = w_in.shape[1]
    rw_cols = n_cols - 3 * DA_WIDTH
    tm = _pick_tile(T, 512)
    row = lambda i: (i, 0)
    fixed = lambda i: (0, 0)
    return pl.pallas_call(
        _inproj_body,
        grid=(T // tm,),
        in_specs=[pl.BlockSpec((tm, D), row),
                  pl.BlockSpec((1, D), fixed),
                  pl.BlockSpec((D, n_cols), fixed),
                  pl.BlockSpec((tm, DA_V_DIM), row),
                  pl.BlockSpec((tm, DA_V_DIM), row)],
        out_specs=[pl.BlockSpec((tm, DA_WIDTH), row),
                   pl.BlockSpec((tm, DA_WIDTH), row),
                   pl.BlockSpec((tm, DA_WIDTH), row),
                   pl.BlockSpec((tm, rw_cols), row)],
        out_shape=[jax.ShapeDtypeStruct((T, DA_WIDTH), BF16),
                   jax.ShapeDtypeStruct((T, DA_WIDTH), BF16),
                   jax.ShapeDtypeStruct((T, DA_WIDTH), BF16),
                   jax.ShapeDtypeStruct((T, rw_cols), F32)],
        compiler_params=_cparams("parallel"),
        name="input_projection",
    )(x2, norm_w.reshape(1, D), w_in.astype(BF16), cos_t, sin_t)


def _rope_tables(positions):
    half = DA_HEAD_DIM // 2
    inv = ROPE_THETA ** (-jnp.arange(0, DA_HEAD_DIM, 2, dtype=F32) / DA_HEAD_DIM)
    ang = positions.reshape(-1).astype(F32)[:, None] * inv
    cos, sin = jnp.cos(ang), jnp.sin(ang)
    cos_t = jnp.tile(cos, (1, DA_V_DIM // half))
    sin_t = jnp.tile(jnp.concatenate([-sin, sin], axis=-1), (1, DA_V_DIM // DA_HEAD_DIM))
    return cos_t, sin_t


ATTN_Q_BLOCK = 1024
ATTN_K_BLOCK = 512
ATTN_Q_SUB = 128


def _diff_attn_body(q_ref, k_ref, v_ref, lq1_ref, lk1_ref, lq2_ref, lk2_ref, sw_ref, o_ref,
                    m_ref, acc_ref, *, tq, tk, q_sub):
    qi = pl.program_id(2)
    q = q_ref[0]
    lane = lax.broadcasted_iota(jnp.int32, q.shape, 1)
    zero = jnp.zeros_like(q)
    q_heads = (jnp.where(lane < DA_HEAD_DIM, q, zero), jnp.where(lane >= DA_HEAD_DIM, q, zero))

    m_ref[...] = jnp.full(m_ref.shape, -jnp.inf, F32)
    acc_ref[...] = jnp.zeros(acc_ref.shape, F32)

    def step(j, diag):
        ks = pl.multiple_of(j * tk, tk)
        kb = k_ref[0, pl.ds(ks, tk), :]
        vb = v_ref[0, pl.ds(ks, tk), :]
        v_ext = jnp.concatenate([vb, jnp.ones_like(vb)], axis=1)
        for r0 in range(0, tq, q_sub):
            rows = slice(r0, r0 + q_sub)
            n_keys, masked = tk, False
            if diag is not None:
                k0 = diag * tk
                n_keys = min(tk, r0 + q_sub - k0)
                if n_keys <= 0:
                    continue
                masked = k0 + n_keys - 1 > r0
            for c in range(2):
                s = lax.dot_general(q_heads[c][rows], kb[:n_keys], (((1,), (1,)), ((), ())),
                                    preferred_element_type=F32)
                if masked:
                    ri = lax.broadcasted_iota(jnp.int32, s.shape, 0) + r0
                    ci = lax.broadcasted_iota(jnp.int32, s.shape, 1) + k0
                    s = jnp.where(ci <= ri, s, -jnp.inf)
                m_prev = m_ref[c, rows, :]
                m_new = jnp.maximum(m_prev, jnp.max(s, axis=-1, keepdims=True))
                alpha = jnp.exp(m_prev - m_new)
                p = jnp.concatenate(
                    [jnp.exp(s[:, t * V7X_LANES:(t + 1) * V7X_LANES] - m_new).astype(vb.dtype)
                     for t in range(n_keys // V7X_LANES)], axis=1)
                pv = jnp.dot(p, v_ext[:n_keys], preferred_element_type=F32)
                acc_ref[c, rows, :] = jnp.concatenate([alpha, alpha], axis=1) * acc_ref[c, rows, :] + pv
                m_ref[c, rows, :] = m_new

    def full_step(j, carry):
        step(j, None)
        return carry

    lax.fori_loop(0, qi * (tq // tk), full_step, 0)
    for d in range(tq // tk):
        step(qi * (tq // tk) + d, d)

    lam = (jnp.exp(jnp.sum(lq1_ref[...] * lk1_ref[...], axis=-1, keepdims=True))
           - jnp.exp(jnp.sum(lq2_ref[...] * lk2_ref[...], axis=-1, keepdims=True)) + LAM_INIT)
    a0, a1 = acc_ref[0], acc_ref[1]
    o = a0[:, :DA_V_DIM] / a0[:, DA_V_DIM:] - lam * (a1[:, :DA_V_DIM] / a1[:, DA_V_DIM:])
    o = o * lax.rsqrt(jnp.mean(o * o, axis=-1, keepdims=True) + NORM_EPS)
    o_ref[0] = (o * sw_ref[...] * (1.0 - LAM_INIT)).astype(o_ref.dtype)


def _diff_attention(q, k, v, lq1, lk1, lq2, lk2, subln_w, B, S):
    tq = _pick_tile(S, ATTN_Q_BLOCK)
    tk = _pick_tile(tq, ATTN_K_BLOCK)
    q_sub = _pick_tile(tk, ATTN_Q_SUB)
    q3, k3, v3 = (t.reshape(B, S, DA_WIDTH) for t in (q, k, v))
    vec = lambda a: a.reshape(1, -1).astype(F32)
    blk = lambda b, h, i: (b, i, h)
    seq = lambda b, h, i: (b, 0, h)
    fixed = lambda b, h, i: (0, 0)
    out = pl.pallas_call(
        functools.partial(_diff_attn_body, tq=tq, tk=tk, q_sub=q_sub),
        grid=(B, DA_HEADS, S // tq),
        in_specs=[pl.BlockSpec((1, tq, DA_V_DIM), blk),
                  pl.BlockSpec((1, S, DA_V_DIM), seq),
                  pl.BlockSpec((1, S, DA_V_DIM), seq),
                  pl.BlockSpec((1, DA_HEAD_DIM), fixed),
                  pl.BlockSpec((1, DA_HEAD_DIM), fixed),
                  pl.BlockSpec((1, DA_HEAD_DIM), fixed),
                  pl.BlockSpec((1, DA_HEAD_DIM), fixed),
                  pl.BlockSpec((1, DA_V_DIM), fixed)],
        out_specs=pl.BlockSpec((1, tq, DA_V_DIM), blk),
        out_shape=jax.ShapeDtypeStruct((B, S, DA_WIDTH), BF16),
        scratch_shapes=[pltpu.VMEM((2, tq, V7X_LANES), F32),
                        pltpu.VMEM((2, tq, 2 * DA_V_DIM), F32)],
        compiler_params=_cparams("parallel", "parallel", "arbitrary"),
        name="diff_attention",
    )(q3, k3, v3, vec(lq1), vec(lk1), vec(lq2), vec(lk2), vec(subln_w))
    return out.reshape(B * S, DA_WIDTH)


RW_CHUNK = 64
RW_PAIR = 2 * RW_HEAD
RW_INV_DOUBLINGS = 5


def _bdot(a, b):
    return jnp.dot(a.astype(BF16), b.astype(BF16), preferred_element_type=F32)


def _bdot_nt(a, b):
    return lax.dot_general(a.astype(BF16), b.astype(BF16), (((1,), (1,)), ((), ())),
                           preferred_element_type=F32)


def _bdot_tn(a, b):
    return lax.dot_general(a.astype(BF16), b.astype(BF16), (((0,), (0,)), ((), ())),
                           preferred_element_type=F32)


def _split3_dot(m, x):
    mb = m.astype(BF16)
    x0 = x.astype(BF16)
    r1 = x - x0.astype(F32)
    x1 = r1.astype(BF16)
    x2 = (r1 - x1.astype(F32)).astype(BF16)
    dot = lambda t: jnp.dot(mb, t, preferred_element_type=F32)
    return dot(x0) + dot(x1) + dot(x2)


def _rwkv_body(p_ref, prev_ref, mu_ref, w0_ref, a0_ref, kk_ref, ka_ref, rk_ref, lnw_ref, lnb_ref,
               wa_ref, g2_ref, o_ref,
               s_ref, rt_ref, kh_ref, at_ref, bh_ref, v_ref, gl_ref, y_ref, *, tb, width):
    i = pl.program_id(1)
    n_pairs = width // RW_PAIR
    C = RW_CHUNK

    @pl.when(i == 0)
    def _():
        s_ref[...] = jnp.zeros(s_ref.shape, F32)

    p = p_ref[0]
    row = lax.broadcasted_iota(jnp.int32, p.shape, 0)
    last_prev = jnp.where(i == 0, 0.0, prev_ref[0, 7:8, :])
    prev = jnp.where(row == 0, last_prev, pltpu.roll(p, 1, 0))
    ps = p + mu_ref[...] * (prev - p)
    r = ps[:, 0:width]
    k = ps[:, width:2 * width]
    v = ps[:, 2 * width:3 * width]
    c3 = 3 * width
    wa_in = ps[:, c3:c3 + DECAY_LORA + AAA_LORA]
    lane = lax.broadcasted_iota(jnp.int32, wa_in.shape, 1)
    wa_in = jnp.where(lane < DECAY_LORA, jnp.tanh(wa_in), wa_in)
    wa = _bdot(wa_in, wa_ref[...])
    gd = ps[:, c3 + DECAY_LORA + AAA_LORA:]
    g = _bdot(jax.nn.sigmoid(gd), g2_ref[...])

    wl = w0_ref[...] + wa[:, :width]
    softplus_neg = jnp.maximum(-wl, 0.0) + jnp.log1p(jnp.exp(-jnp.abs(wl)))
    logw = -jnp.exp(-softplus_neg - 0.5)
    a = jax.nn.sigmoid(a0_ref[...] + wa[:, width:])

    hr = lax.broadcasted_iota(jnp.int32, (width, width), 0) // RW_HEAD
    hc = lax.broadcasted_iota(jnp.int32, (width, width), 1) // RW_HEAD
    head_ones = jnp.where(hr == hc, 1.0, 0.0).astype(BF16)
    head_sum = lambda t: jnp.dot(t.astype(BF16), head_ones, preferred_element_type=F32)

    kk = k * kk_ref[...]
    kk = kk / jnp.maximum(jnp.sqrt(head_sum(kk * kk)), 1e-12)
    k2 = k * (1.0 + (a - 1.0) * ka_ref[...])
    bonus = head_sum(r * k2 * rk_ref[...]) * v

    tr = lax.broadcasted_iota(jnp.int32, (tb, tb), 0)
    tc = lax.broadcasted_iota(jnp.int32, (tb, tb), 1)
    chunk_tril = jnp.where((tr // C == tc // C) & (tc <= tr), 1.0, 0.0)
    cum = _split3_dot(chunk_tril, logw)
    g_inc = jnp.exp(cum)
    g_inv = jnp.exp(-cum)
    g_exc = jnp.exp(cum - logw)
    rt_ref[...] = r * g_inc
    kh_ref[...] = k2 * g_inv
    at_ref[...] = -kk * g_exc
    bh_ref[...] = kk * a * g_inv
    v_ref[...] = v
    gl_ref[...] = g_inc

    lane_p = lax.broadcasted_iota(jnp.int32, (C, RW_PAIR), 1)
    first = lane_p < RW_HEAD
    ri = lax.broadcasted_iota(jnp.int32, (RW_PAIR, RW_PAIR), 0)
    ci = lax.broadcasted_iota(jnp.int32, (RW_PAIR, RW_PAIR), 1)
    strict = ci < ri
    incl = ci <= ri
    eye = jnp.where(ci == ri, 1.0, 0.0)

    def stack(t):
        return jnp.concatenate([jnp.where(first, t, 0.0), jnp.where(first, 0.0, t)], axis=0)

    n_chunks = tb // C
    units = [(c, j) for c in range(n_chunks) for j in range(n_pairs)]
    U = range(len(units))
    rows = [slice(c * C, (c + 1) * C) for c, _ in units]
    lanes = [slice(j * RW_PAIR, (j + 1) * RW_PAIR) for _, j in units]
    rts = [stack(rt_ref[rows[n], lanes[n]]) for n in U]
    khs = [stack(kh_ref[rows[n], lanes[n]]) for n in U]
    ats = [stack(at_ref[rows[n], lanes[n]]) for n in U]
    bhs = [stack(bh_ref[rows[n], lanes[n]]) for n in U]
    vs = [stack(v_ref[rows[n], lanes[n]]) for n in U]
    g_last = [gl_ref[(c + 1) * C - 1:(c + 1) * C, lanes[n]] for n, (c, _) in enumerate(units)]

    sc = [_bdot_nt(jnp.concatenate([ats[n], rts[n]], axis=0), jnp.concatenate([bhs[n], khs[n]], axis=0))
          for n in U]
    l_ab = [jnp.where(strict, sc[n][:RW_PAIR, :RW_PAIR], 0.0) for n in U]
    l_ak = [jnp.where(strict, sc[n][:RW_PAIR, RW_PAIR:], 0.0) for n in U]
    a_rb = [jnp.where(incl, sc[n][RW_PAIR:, :RW_PAIR], 0.0) for n in U]
    a_rk = [jnp.where(incl, sc[n][RW_PAIR:, RW_PAIR:], 0.0) for n in U]

    pw = l_ab
    tinv = [eye + l_ab[n] for n in U]
    for _ in range(RW_INV_DOUBLINGS):
        pw = [_bdot(pw[n], pw[n]) for n in U]
        tinv = [tinv[n] + _bdot(tinv[n], pw[n]) for n in U]

    lv = [_bdot(l_ak[n], vs[n]) for n in U]
    g0 = [_bdot_tn(vs[n], khs[n]) for n in U]
    taw = [_bdot(tinv[n], jnp.concatenate([ats[n], lv[n]], axis=1)) for n in U]
    ar = [_bdot(jnp.concatenate([a_rb[n], a_rk[n]], axis=1),
                jnp.concatenate([taw[n], jnp.concatenate([jnp.zeros_like(vs[n]), vs[n]], axis=1)], axis=0))
          for n in U]
    qt_ta = [jnp.concatenate([rts[n] + ar[n][:, :RW_PAIR], taw[n][:, :RW_PAIR]], axis=0) for n in U]

    state = [s_ref[j] for j in range(n_pairs)]
    for c in range(n_chunks):
        ns = [c * n_pairs + j for j in range(n_pairs)]
        m1 = [_bdot_nt(qt_ta[n], state[j]) for j, n in enumerate(ns)]
        u = [m1[j][RW_PAIR:] + taw[n][:, RW_PAIR:] for j, n in enumerate(ns)]
        state = [(state[j] + g0[n] + _bdot_tn(u[j], bhs[n])) * g_last[n] for j, n in enumerate(ns)]
        for j, n in enumerate(ns):
            y = m1[j][:RW_PAIR] + ar[n][:, RW_PAIR:]
            y_ref[rows[n], lanes[n]] = y[:C] + y[C:]
    for j in range(n_pairs):
        s_ref[j] = state[j]

    y = y_ref[...]
    inv_n = 1.0 / RW_HEAD
    mean = head_sum(y) * inv_n
    yc = y - mean
    var = head_sum(yc * yc) * inv_n
    yn = yc * lax.rsqrt(var + GN_EPS) * lnw_ref[...] + lnb_ref[...]
    o_ref[0] = ((yn + bonus) * g).astype(o_ref.dtype)


def _rwkv7(p_rw, shift_mu, w0, w2, a0, a2, g2, k_k, k_a, r_k, lnx_w, lnx_b, B, S):
    cols = p_rw.shape[-1]
    width = (cols - DECAY_LORA - AAA_LORA - GATE_LORA) // 3
    tb = _pick_tile(S, 256)
    assert tb % RW_CHUNK == 0 and width % RW_PAIR == 0
    p3 = p_rw.reshape(B, S, cols)
    vec = lambda t: t.reshape(1, -1).astype(F32)
    zeros = jnp.zeros_like(w2)
    wa = jnp.concatenate([jnp.concatenate([w2, zeros], axis=1),
                          jnp.concatenate([jnp.zeros_like(a2), a2], axis=1)], axis=0).astype(BF16)
    blk = lambda b, i: (b, i, 0)
    prev_blk = lambda b, i: (b, jnp.maximum(i * (tb // 8) - 1, 0), 0)
    fixed = lambda b, i: (0, 0)
    wide = pltpu.VMEM((tb, width), F32)
    out = pl.pallas_call(
        functools.partial(_rwkv_body, tb=tb, width=width),
        grid=(B, S // tb),
        in_specs=[pl.BlockSpec((1, tb, cols), blk),
                  pl.BlockSpec((1, 8, cols), prev_blk),
                  pl.BlockSpec((1, cols), fixed)]
                 + [pl.BlockSpec((1, width), fixed)] * 7
                 + [pl.BlockSpec((DECAY_LORA + AAA_LORA, 2 * width), fixed),
                    pl.BlockSpec((GATE_LORA, width), fixed)],
        out_specs=pl.BlockSpec((1, tb, width), blk),
        out_shape=jax.ShapeDtypeStruct((B, S, width), BF16),
        scratch_shapes=[pltpu.VMEM((width // RW_PAIR, RW_PAIR, RW_PAIR), F32)] + [wide] * 7,
        compiler_params=_cparams("parallel", "arbitrary"),
        name="rwkv7_mix",
    )(p3, p3, vec(shift_mu), vec(w0), vec(a0), vec(k_k), vec(k_a), vec(r_k), vec(lnx_w), vec(lnx_b),
      wa, g2.astype(BF16))
    return out.reshape(B * S, width)


def _outproj_router_body(oda_ref, orw_ref, x_ref, wa_ref, wb_ref, nw_ref, rw_ref, rb_ref,
                         x1_ref, h_ref, idx_ref, gate_ref, rank_ref, cnt_ref, carry_ref):
    i = pl.program_id(0)

    @pl.when(i == 0)
    def _():
        carry_ref[...] = jnp.zeros(carry_ref.shape, F32)

    x1 = (x_ref[...] + jnp.dot(oda_ref[...], wa_ref[...], preferred_element_type=F32)
          + jnp.dot(orw_ref[...], wb_ref[...], preferred_element_type=F32))
    x1_ref[...] = x1
    h = (x1 * lax.rsqrt(jnp.mean(x1 * x1, axis=-1, keepdims=True) + NORM_EPS)) * nw_ref[...]
    h_ref[...] = h
    logits = jnp.dot(h, rw_ref[...], preferred_element_type=F32,
                     precision=lax.Precision.HIGHEST) + rb_ref[...]
    tm, n_exp = logits.shape
    lane = lax.broadcasted_iota(jnp.int32, logits.shape, 1).astype(F32)
    out_lane = lax.broadcasted_iota(jnp.int32, (tm, TOP_K), 1)

    vals, idxs = [], []
    rest = logits
    for _ in range(TOP_K):
        m = jnp.max(rest, axis=-1, keepdims=True)
        sel = jnp.min(jnp.where(rest == m, lane, float(n_exp)), axis=-1, keepdims=True)
        vals.append(m)
        idxs.append(sel)
        rest = jnp.where(lane == sel, -jnp.inf, rest)
    exps = [jnp.exp(v - vals[0]) for v in vals]
    denom = exps[0]
    for e in exps[1:]:
        denom = denom + e

    hot = [jnp.where(lane == s, 1.0, 0.0) for s in idxs]
    hot_all = hot[0]
    for t in hot[1:]:
        hot_all = hot_all + t
    tr = lax.broadcasted_iota(jnp.int32, (tm, tm), 0)
    tc = lax.broadcasted_iota(jnp.int32, (tm, tm), 1)
    before = jnp.dot(jnp.where(tc < tr, 1.0, 0.0).astype(BF16), hot_all.astype(BF16),
                     preferred_element_type=F32) + carry_ref[...]

    idx_out = jnp.zeros((tm, TOP_K), F32)
    gate_out = jnp.zeros((tm, TOP_K), F32)
    rank_out = jnp.zeros((tm, TOP_K), F32)
    for kk in range(TOP_K):
        rank_k = jnp.sum(hot[kk] * before, axis=-1, keepdims=True)
        idx_out = jnp.where(out_lane == kk, idxs[kk], idx_out)
        gate_out = jnp.where(out_lane == kk, exps[kk] / denom, gate_out)
        rank_out = jnp.where(out_lane == kk, rank_k, rank_out)
    idx_ref[...] = idx_out.astype(jnp.int32)
    gate_ref[...] = gate_out
    rank_ref[...] = rank_out.astype(jnp.int32)
    carry_ref[...] += jnp.sum(hot_all, axis=0, keepdims=True)
    cnt_ref[...] = carry_ref[...]


def _outproj_router(o_da, o_rw, x2, w_out, ffn_norm_w, router_w, router_b):
    T, D = x2.shape
    n_exp = router_w.shape[1]
    tm = _pick_tile(T, 512)
    wa = w_out[:DA_WIDTH].astype(BF16)
    wb = w_out[DA_WIDTH:].astype(BF16)
    row = lambda i: (i, 0)
    fixed = lambda i: (0, 0)
    return pl.pallas_call(
        _outproj_router_body,
        grid=(T // tm,),
        in_specs=[pl.BlockSpec((tm, DA_WIDTH), row),
                  pl.BlockSpec((tm, o_rw.shape[1]), row),
                  pl.BlockSpec((tm, D), row),
                  pl.BlockSpec(wa.shape, fixed),
                  pl.BlockSpec(wb.shape, fixed),
                  pl.BlockSpec((1, D), fixed),
                  pl.BlockSpec((D, n_exp), fixed),
                  pl.BlockSpec((1, n_exp), fixed)],
        out_specs=[pl.BlockSpec((tm, D), row),
                   pl.BlockSpec((tm, D), row),
                   pl.BlockSpec((tm, TOP_K), row),
                   pl.BlockSpec((tm, TOP_K), row),
                   pl.BlockSpec((tm, TOP_K), row),
                   pl.BlockSpec((1, n_exp), fixed)],
        out_shape=[jax.ShapeDtypeStruct((T, D), F32),
                   jax.ShapeDtypeStruct((T, D), F32),
                   jax.ShapeDtypeStruct((T, TOP_K), jnp.int32),
                   jax.ShapeDtypeStruct((T, TOP_K), F32),
                   jax.ShapeDtypeStruct((T, TOP_K), jnp.int32),
                   jax.ShapeDtypeStruct((1, n_exp), F32)],
        scratch_shapes=[pltpu.VMEM((1, n_exp), F32)],
        compiler_params=_cparams("arbitrary"),
        name="outproj_router",
    )(o_da, o_rw, x2, wa, wb, ffn_norm_w.reshape(1, D), router_w, router_b.reshape(1, n_exp))


MOE_ROWS = 256
DISPATCH_TOKENS = 256
ISSUE_UNROLL = 8
MXU_TILE = 256


def _dispatch_body(dest_ref, h_ref, init_hbm, xs_hbm, sem):
    del init_hbm

    def issue(t, carry):
        src = h_ref.at[pl.ds(t, 1)]
        for kk in range(TOP_K):
            pltpu.make_async_copy(src, xs_hbm.at[pl.ds(dest_ref[0, 0, t * TOP_K + kk], 1)], sem).start()
        return carry

    lax.fori_loop(0, DISPATCH_TOKENS, issue, 0, unroll=ISSUE_UNROLL)
    for _ in range(TOP_K):
        pltpu.make_async_copy(h_ref, xs_hbm.at[pl.ds(0, DISPATCH_TOKENS)], sem).wait()


def _dispatch(h, dest, n_padded):
    T, D = h.shape
    n_blocks = T // DISPATCH_TOKENS
    dest3 = dest.reshape(n_blocks, 1, DISPATCH_TOKENS * TOP_K)
    return pl.pallas_call(
        _dispatch_body,
        grid=(n_blocks,),
        in_specs=[pl.BlockSpec((1, 1, DISPATCH_TOKENS * TOP_K), lambda i: (i, 0, 0), memory_space=pltpu.SMEM),
                  pl.BlockSpec((DISPATCH_TOKENS, D), lambda i: (i, 0)),
                  pl.BlockSpec(memory_space=pl.ANY)],
        out_specs=pl.BlockSpec(memory_space=pl.ANY),
        out_shape=jax.ShapeDtypeStruct((n_padded, D), h.dtype),
        scratch_shapes=[pltpu.SemaphoreType.DMA(())],
        input_output_aliases={2: 0},
        compiler_params=_cparams("arbitrary"),
        name="moe_dispatch",
    )(dest3, h, jnp.zeros((n_padded, D), h.dtype))


def _mlp1_regroup_body(w_ref, o_ref):
    half = MXU_TILE // 2
    r = lax.broadcasted_iota(jnp.int32, (MXU_TILE, MXU_TILE), 0)
    c = lax.broadcasted_iota(jnp.int32, (MXU_TILE, MXU_TILE), 1)
    perm = jnp.where(c == (r >> 1) + (r & 1) * half, 1.0, 0.0).astype(BF16)
    for g in range(w_ref.shape[2] // MXU_TILE):
        sl = slice(g * MXU_TILE, (g + 1) * MXU_TILE)
        o_ref[0, :, sl] = jnp.dot(w_ref[0, :, sl].astype(BF16), perm,
                                  preferred_element_type=F32).astype(o_ref.dtype)


def _mlp1_regroup(mlp1_w):
    n_exp, D, cols = mlp1_w.shape
    tc = _pick_tile(cols, 2 * MXU_TILE)
    blk = lambda e, j: (e, 0, j)
    return pl.pallas_call(
        _mlp1_regroup_body,
        grid=(n_exp, cols // tc),
        in_specs=[pl.BlockSpec((1, D, tc), blk)],
        out_specs=pl.BlockSpec((1, D, tc), blk),
        out_shape=jax.ShapeDtypeStruct(mlp1_w.shape, BF16),
        compiler_params=_cparams("parallel", "parallel"),
        name="mlp1_regroup",
    )(mlp1_w)


def _experts_body(be_ref, nu_ref, xs_ref, w1_ref, b1_ref, w2_ref, b2_ref, y_ref):
    del be_ref
    i = pl.program_id(0)
    half = MXU_TILE // 2

    @pl.when(i < nu_ref[0])
    def _():
        x = xs_ref[...].astype(BF16)
        gu = jnp.dot(x, w1_ref[0], preferred_element_type=F32) + b1_ref[0]
        acts = []
        for g in range(gu.shape[1] // MXU_TILE):
            gate = jnp.minimum(gu[:, g * MXU_TILE:g * MXU_TILE + half], SWIGLU_LIMIT)
            up = jnp.clip(gu[:, g * MXU_TILE + half:(g + 1) * MXU_TILE], -SWIGLU_LIMIT, SWIGLU_LIMIT)
            acts.append(((up + 1.0) * gate * jax.nn.sigmoid(SWIGLU_ALPHA * gate)).astype(BF16))
        act = jnp.concatenate(acts, axis=1)
        y_ref[...] = jnp.dot(act, w2_ref[0], preferred_element_type=F32) + b2_ref[0]

    @pl.when(i >= nu_ref[0])
    def _():
        y_ref[...] = jnp.zeros(y_ref.shape, y_ref.dtype)


def _experts(xs, blk_expert, n_used, w1, b1, w2, b2):
    n_padded, D = xs.shape
    n_blocks = n_padded // MOE_ROWS
    gu_cols = w1.shape[2]
    xrow = lambda i, be, nu: (jnp.maximum(jnp.minimum(i, nu[0] - 1), 0), 0)
    yrow = lambda i, be, nu: (i, 0)
    wsel = lambda i, be, nu: (be[i], 0, 0)
    return pl.pallas_call(
        _experts_body,
        grid_spec=pltpu.PrefetchScalarGridSpec(
            num_scalar_prefetch=2,
            grid=(n_blocks,),
            in_specs=[pl.BlockSpec((MOE_ROWS, D), xrow),
                      pl.BlockSpec((1, D, gu_cols), wsel),
                      pl.BlockSpec((1, 1, gu_cols), wsel),
                      pl.BlockSpec((1, gu_cols // 2, D), wsel),
                      pl.BlockSpec((1, 1, D), wsel)],
            out_specs=pl.BlockSpec((MOE_ROWS, D), yrow)),
        out_shape=jax.ShapeDtypeStruct((n_padded, D), F32),
        compiler_params=_cparams("arbitrary"),
        name="moe_experts",
    )(blk_expert, n_used, xs, w1, b1, w2, b2)


def _combine_body(dest_ref, next_ref, ys_hbm, x1_ref, gate_ref, fw_ref, o_ref, buf_ref, sems):
    i = pl.program_id(0)
    slot = i % 2

    def gather(d_ref, s):
        def issue(t, carry):
            for kk in range(TOP_K):
                pltpu.make_async_copy(ys_hbm.at[pl.ds(d_ref[0, 0, t * TOP_K + kk], 1)],
                                      buf_ref.at[s, kk, pl.ds(t, 1)], sems.at[s]).start()
            return carry

        lax.fori_loop(0, DISPATCH_TOKENS, issue, 0, unroll=ISSUE_UNROLL)

    @pl.when(i == 0)
    def _():
        gather(dest_ref, 0)

    @pl.when(i + 1 < pl.num_programs(0))
    def _():
        gather(next_ref, 1 - slot)

    for kk in range(TOP_K):
        pltpu.make_async_copy(ys_hbm.at[pl.ds(0, DISPATCH_TOKENS)], buf_ref.at[slot, kk], sems.at[slot]).wait()

    gates = gate_ref[...]
    x2 = x1_ref[...]
    for kk in range(TOP_K):
        x2 = x2 + gates[:, kk:kk + 1] * buf_ref[slot, kk]
    o_ref[...] = (x2 * lax.rsqrt(jnp.mean(x2 * x2, axis=-1, keepdims=True) + NORM_EPS)) * fw_ref[...]


def _combine(ys, dest, x1, gates, final_w):
    T, D = x1.shape
    n_blocks = T // DISPATCH_TOKENS
    dest3 = dest.reshape(n_blocks, 1, DISPATCH_TOKENS * TOP_K)
    row = lambda i: (i, 0)
    idx_blk = (1, 1, DISPATCH_TOKENS * TOP_K)
    return pl.pallas_call(
        _combine_body,
        grid=(n_blocks,),
        in_specs=[pl.BlockSpec(idx_blk, lambda i: (i, 0, 0), memory_space=pltpu.SMEM),
                  pl.BlockSpec(idx_blk, lambda i: (jnp.minimum(i + 1, n_blocks - 1), 0, 0),
                               memory_space=pltpu.SMEM),
                  pl.BlockSpec(memory_space=pl.ANY),
                  pl.BlockSpec((DISPATCH_TOKENS, D), row),
                  pl.BlockSpec((DISPATCH_TOKENS, TOP_K), row),
                  pl.BlockSpec((1, D), lambda i: (0, 0))],
        out_specs=pl.BlockSpec((DISPATCH_TOKENS, D), row),
        out_shape=jax.ShapeDtypeStruct((T, D), F32),
        scratch_shapes=[pltpu.VMEM((2, TOP_K, DISPATCH_TOKENS, D), F32), pltpu.SemaphoreType.DMA((2,))],
        compiler_params=_cparams("arbitrary"),
        name="moe_combine",
    )(dest3, dest3, ys, x1, gates, final_w.reshape(1, D))


def _moe(h, x1, idx, gates, rank, counts, mlp1_w, mlp1_b, mlp2_w, mlp2_b, final_w):
    T, D = h.shape
    n_exp = mlp1_w.shape[0]
    assert T % DISPATCH_TOKENS == 0
    n_padded = T * TOP_K + n_exp * MOE_ROWS
    n_blocks = n_padded // MOE_ROWS
    counts = counts.reshape(n_exp).astype(jnp.int32)
    padded = (counts + MOE_ROWS - 1) // MOE_ROWS * MOE_ROWS
    pend = jnp.cumsum(padded)
    pstart = pend - padded
    dest = (pstart[idx] + rank).astype(jnp.int32)
    starts = jnp.arange(n_blocks, dtype=jnp.int32) * MOE_ROWS
    blk_expert = jnp.minimum(jnp.sum((pend[None, :] <= starts[:, None]).astype(jnp.int32), axis=1),
                             n_exp - 1).astype(jnp.int32)
    n_used = (pend[-1:] // MOE_ROWS).astype(jnp.int32)

    w1 = _mlp1_regroup(mlp1_w)
    half = MXU_TILE // 2
    b1 = mlp1_b.reshape(n_exp, -1, half, 2).transpose(0, 1, 3, 2).reshape(n_exp, 1, -1)
    xs = _dispatch(h, dest, n_padded)
    ys = _experts(xs, blk_expert, n_used, w1, b1, mlp2_w.astype(BF16), mlp2_b[:, None, :])
    return _combine(ys, dest, x1, gates, final_w)


def kernel(x, positions, attn_norm_w, w_in, shift_mu, lambda_q1, lambda_k1, lambda_q2, lambda_k2, subln_w,
           rw_w0, rw_w2, rw_a0, rw_a2, rw_g2, rw_k_k, rw_k_a, rw_r_k, rw_lnx_w, rw_lnx_b, w_out, ffn_norm_w,
           router_w, router_b, mlp1_w, mlp1_b, mlp2_w, mlp2_b, final_norm_w):
    B, S, D = x.shape
    assert attn_norm_w.shape[0] == 1, "single-layer stack"
    l = LAYER_INDEX
    x2 = x.reshape(B * S, D)
    cos_t, sin_t = _rope_tables(positions)
    q, k, v, p_rw = _input_projection(x2, attn_norm_w[l], w_in[l], cos_t, sin_t)
    o_da = _diff_attention(q, k, v, lambda_q1[l], lambda_k1[l], lambda_q2[l], lambda_k2[l], subln_w[l], B, S)
    o_rw = _rwkv7(p_rw, shift_mu[l], rw_w0[l], rw_w2[l], rw_a0[l], rw_a2[l], rw_g2[l],
                  rw_k_k[l], rw_k_a[l], rw_r_k[l], rw_lnx_w[l], rw_lnx_b[l], B, S)
    x1, h, idx, gates, rank, counts = _outproj_router(o_da, o_rw, x2, w_out[l], ffn_norm_w[l],
                                                      router_w[l], router_b[l])
    out = _moe(h, x1, idx, gates, rank, counts, mlp1_w[l], mlp1_b[l], mlp2_w[l], mlp2_b[l], final_norm_w)
    return out.reshape(B, S, D)
```

```python
import functools
import math

import jax
import jax.numpy as jnp
from jax import lax
from jax.experimental import pallas as pl
from jax.experimental.pallas import tpu as pltpu

F32 = jnp.float32
BF16 = jnp.bfloat16

V7X_LANES = 128
V7X_VMEM_BYTES = 64 * 1024 * 1024
VMEM_LIMIT_BYTES = 56 * 1024 * 1024

DA_HEADS = 4
DA_HEAD_DIM = 64
DA_V_DIM = 2 * DA_HEAD_DIM
DA_WIDTH = DA_HEADS * DA_V_DIM
RW_HEAD = 64
DECAY_LORA = 64
AAA_LORA = 64
GATE_LORA = 128
ROPE_THETA = 10000.0
N_EXPERTS = 32
TOP_K = 4
SWIGLU_ALPHA = 1.702
SWIGLU_LIMIT = 7.0
NORM_EPS = 1e-5
GN_EPS = 64e-5
LAYER_INDEX = 0
LAM_INIT = 0.8 - 0.6 * math.exp(-0.3 * LAYER_INDEX)


def _cparams(*semantics):
    return pltpu.CompilerParams(dimension_semantics=semantics, vmem_limit_bytes=VMEM_LIMIT_BYTES)


def _pick_tile(n, want):
    t = min(n, want)
    while n % t:
        t //= 2
    return t


def _inproj_body(x_ref, nw_ref, w_ref, cos_ref, sin_ref, q_ref, k_ref, v_ref, prw_ref):
    x = x_ref[...]
    ms = jnp.mean(x * x, axis=-1, keepdims=True)
    h = (x * lax.rsqrt(ms + NORM_EPS)) * nw_ref[...]
    p = jnp.dot(h.astype(BF16), w_ref[...], preferred_element_type=F32)
    cos = cos_ref[...]
    sin = sin_ref[...]
    lane = lax.broadcasted_iota(jnp.int32, cos.shape, 1)
    first_half = (lane % DA_HEAD_DIM) < (DA_HEAD_DIM // 2)
    scale = DA_HEAD_DIM ** -0.5

    def rope(t):
        partner = jnp.where(first_half,
                            pltpu.roll(t, V7X_LANES - DA_HEAD_DIM // 2, 1),
                            pltpu.roll(t, DA_HEAD_DIM // 2, 1))
        return t * cos + partner * sin

    for hh in range(DA_HEADS):
        sl = slice(hh * DA_V_DIM, (hh + 1) * DA_V_DIM)
        q_ref[:, sl] = (rope(p[:, sl]) * scale).astype(q_ref.dtype)
        ksl = slice(DA_WIDTH + hh * DA_V_DIM, DA_WIDTH + (hh + 1) * DA_V_DIM)
        k_ref[:, sl] = rope(p[:, ksl]).astype(k_ref.dtype)
    v_ref[...] = p[:, 2 * DA_WIDTH:3 * DA_WIDTH].astype(v_ref.dtype)
    prw_ref[...] = p[:, 3 * DA_WIDTH:]


def _input_projection(x2, norm_w, w_in, cos_t, sin_t):
    T, D = x2.shape
    n_cols = w_in.shape[1]
    rw_cols = n_cols - 3 * DA_WIDTH
    tm = _pick_tile(T, 512)
    row = lambda i: (i, 0)
    fixed = lambda i: (0, 0)
    return pl.pallas_call(
        _inproj_body,
        grid=(T // tm,),
        in_specs=[pl.BlockSpec((tm, D), row),
                  pl.BlockSpec((1, D), fixed),
                  pl.BlockSpec((D, n_cols), fixed),
                  pl.BlockSpec((tm, DA_V_DIM), row),
                  pl.BlockSpec((tm, DA_V_DIM), row)],
        out_specs=[pl.BlockSpec((tm, DA_WIDTH), row),
                   pl.BlockSpec((tm, DA_WIDTH), row),
                   pl.BlockSpec((tm, DA_WIDTH), row),
                   pl.BlockSpec((tm, rw_cols), row)],
        out_shape=[jax.ShapeDtypeStruct((T, DA_WIDTH), BF16),
                   jax.ShapeDtypeStruct((T, DA_WIDTH), BF16),
                   jax.ShapeDtypeStruct((T, DA_WIDTH), BF16),
                   jax.ShapeDtypeStruct((T, rw_cols), F32)],
        compiler_params=_cparams("parallel"),
        name="input_projection",
    )(x2, norm_w.reshape(1, D), w_in.astype(BF16), cos_t, sin_t)


def _rope_tables(positions):
    half = DA_HEAD_DIM // 2
    inv = ROPE_THETA ** (-jnp.arange(0, DA_HEAD_DIM, 2, dtype=F32) / DA_HEAD_DIM)
    ang = positions.reshape(-1).astype(F32)[:, None] * inv
    cos, sin = jnp.cos(ang), jnp.sin(ang)
    cos_t = jnp.tile(cos, (1, DA_V_DIM // half))
    sin_t = jnp.tile(jnp.concatenate([-sin, sin], axis=-1), (1, DA_V_DIM // DA_HEAD_DIM))
    return cos_t, sin_t


ATTN_Q_BLOCK = 1024
ATTN_K_BLOCK = 512
ATTN_Q_SUB = 128


def _diff_attn_body(q_ref, k_ref, v_ref, lq1_ref, lk1_ref, lq2_ref, lk2_ref, sw_ref, o_ref,
                    m_ref, acc_ref, *, tq, tk, q_sub):
    qi = pl.program_id(2)
    q = q_ref[0]
    lane = lax.broadcasted_iota(jnp.int32, q.shape, 1)
    zero = jnp.zeros_like(q)
    q_heads = (jnp.where(lane < DA_HEAD_DIM, q, zero), jnp.where(lane >= DA_HEAD_DIM, q, zero))

    m_ref[...] = jnp.full(m_ref.shape, -jnp.inf, F32)
    acc_ref[...] = jnp.zeros(acc_ref.shape, F32)

    def step(j, diag):
        ks = pl.multiple_of(j * tk, tk)
        kb = k_ref[0, pl.ds(ks, tk), :]
        vb = v_ref[0, pl.ds(ks, tk), :]
        v_ext = jnp.concatenate([vb, jnp.ones_like(vb)], axis=1)
        k0 = 0 if diag is None else diag * tk
        units = []
        for r0 in range(0, tq, q_sub):
            n_keys, masked = tk, False
            if diag is not None:
                n_keys = min(tk, r0 + q_sub - k0)
                if n_keys <= 0:
                    continue
                masked = k0 + n_keys - 1 > r0
            units += [(c, r0, n_keys, masked) for c in range(2)]
        scores = [lax.dot_general(q_heads[c][r0:r0 + q_sub], kb[:n_keys], (((1,), (1,)), ((), ())),
                                  preferred_element_type=F32) for c, r0, n_keys, _ in units]
        probs, alphas = [], []
        for (c, r0, n_keys, masked), s in zip(units, scores):
            if masked:
                ri = lax.broadcasted_iota(jnp.int32, s.shape, 0) + r0
                ci = lax.broadcasted_iota(jnp.int32, s.shape, 1) + k0
                s = jnp.where(ci <= ri, s, -jnp.inf)
            m_prev = m_ref[c, r0:r0 + q_sub, :]
            m_new = jnp.maximum(m_prev, jnp.max(s, axis=-1, keepdims=True))
            m_ref[c, r0:r0 + q_sub, :] = m_new
            alphas.append(jnp.exp(m_prev - m_new))
            probs.append(jnp.concatenate(
                [jnp.exp(s[:, t * V7X_LANES:(t + 1) * V7X_LANES] - m_new).astype(vb.dtype)
                 for t in range(n_keys // V7X_LANES)], axis=1))
        for (c, r0, n_keys, _), p, alpha in zip(units, probs, alphas):
            pv = jnp.dot(p, v_ext[:n_keys], preferred_element_type=F32)
            rows = slice(r0, r0 + q_sub)
            acc_ref[c, rows, :] = jnp.concatenate([alpha, alpha], axis=1) * acc_ref[c, rows, :] + pv

    def full_step(j, carry):
        step(j, None)
        return carry

    lax.fori_loop(0, qi * (tq // tk), full_step, 0)
    for d in range(tq // tk):
        step(qi * (tq // tk) + d, d)

    lam = (jnp.exp(jnp.sum(lq1_ref[...] * lk1_ref[...], axis=-1, keepdims=True))
           - jnp.exp(jnp.sum(lq2_ref[...] * lk2_ref[...], axis=-1, keepdims=True)) + LAM_INIT)
    a0, a1 = acc_ref[0], acc_ref[1]
    o = a0[:, :DA_V_DIM] / a0[:, DA_V_DIM:] - lam * (a1[:, :DA_V_DIM] / a1[:, DA_V_DIM:])
    o = o * lax.rsqrt(jnp.mean(o * o, axis=-1, keepdims=True) + NORM_EPS)
    o_ref[0] = (o * sw_ref[...] * (1.0 - LAM_INIT)).astype(o_ref.dtype)


def _diff_attention(q, k, v, lq1, lk1, lq2, lk2, subln_w, B, S):
    tq = _pick_tile(S, ATTN_Q_BLOCK)
    tk = _pick_tile(tq, ATTN_K_BLOCK)
    q_sub = _pick_tile(tk, ATTN_Q_SUB)
    q3, k3, v3 = (t.reshape(B, S, DA_WIDTH) for t in (q, k, v))
    vec = lambda a: a.reshape(1, -1).astype(F32)
    blk = lambda b, h, i: (b, i, h)
    seq = lambda b, h, i: (b, 0, h)
    fixed = lambda b, h, i: (0, 0)
    out = pl.pallas_call(
        functools.partial(_diff_attn_body, tq=tq, tk=tk, q_sub=q_sub),
        grid=(B, DA_HEADS, S // tq),
        in_specs=[pl.BlockSpec((1, tq, DA_V_DIM), blk),
                  pl.BlockSpec((1, S, DA_V_DIM), seq),
                  pl.BlockSpec((1, S, DA_V_DIM), seq),
                  pl.BlockSpec((1, DA_HEAD_DIM), fixed),
                  pl.BlockSpec((1, DA_HEAD_DIM), fixed),
                  pl.BlockSpec((1, DA_HEAD_DIM), fixed),
                  pl.BlockSpec((1, DA_HEAD_DIM), fixed),
                  pl.BlockSpec((1, DA_V_DIM), fixed)],
        out_specs=pl.BlockSpec((1, tq, DA_V_DIM), blk),
        out_shape=jax.ShapeDtypeStruct((B, S, DA_WIDTH), BF16),
        scratch_shapes=[pltpu.VMEM((2, tq, V7X_LANES), F32),
                        pltpu.VMEM((2, tq, 2 * DA_V_DIM), F32)],
        compiler_params=_cparams("parallel", "parallel", "arbitrary"),
        name="diff_attention",
    )(q3, k3, v3, vec(lq1), vec(lk1), vec(lq2), vec(lk2), vec(subln_w))
    return out.reshape(B * S, DA_WIDTH)


RW_CHUNK = 64
RW_PAIR = 2 * RW_HEAD
RW_INV_DOUBLINGS = 5


def _bdot(a, b):
    return jnp.dot(a.astype(BF16), b.astype(BF16), preferred_element_type=F32)


def _bdot_nt(a, b):
    return lax.dot_general(a.astype(BF16), b.astype(BF16), (((1,), (1,)), ((), ())),
                           preferred_element_type=F32)


def _bdot_tn(a, b):
    return lax.dot_general(a.astype(BF16), b.astype(BF16), (((0,), (0,)), ((), ())),
                           preferred_element_type=F32)


def _split3_dot(m, x):
    mb = m.astype(BF16)
    x0 = x.astype(BF16)
    r1 = x - x0.astype(F32)
    x1 = r1.astype(BF16)
    x2 = (r1 - x1.astype(F32)).astype(BF16)
    dot = lambda t: jnp.dot(mb, t, preferred_element_type=F32)
    return dot(x0) + dot(x1) + dot(x2)


def _rwkv_body(p_ref, prev_ref, mu_ref, w0_ref, a0_ref, kk_ref, ka_ref, rk_ref, lnw_ref, lnb_ref,
               wa_ref, g2_ref, o_ref,
               s_ref, rt_ref, kh_ref, at_ref, bh_ref, v_ref, gl_ref, y_ref, *, tb, width):
    i = pl.program_id(1)
    n_pairs = width // RW_PAIR
    C = RW_CHUNK

    @pl.when(i == 0)
    def _():
        s_ref[...] = jnp.zeros(s_ref.shape, F32)

    p = p_ref[0]
    row = lax.broadcasted_iota(jnp.int32, p.shape, 0)
    last_prev = jnp.where(i == 0, 0.0, prev_ref[0, 7:8, :])
    prev = jnp.where(row == 0, last_prev, pltpu.roll(p, 1, 0))
    ps = p + mu_ref[...] * (prev - p)
    r = ps[:, 0:width]
    k = ps[:, width:2 * width]
    v = ps[:, 2 * width:3 * width]
    c3 = 3 * width
    wa_in = ps[:, c3:c3 + DECAY_LORA + AAA_LORA]
    lane = lax.broadcasted_iota(jnp.int32, wa_in.shape, 1)
    wa_in = jnp.where(lane < DECAY_LORA, jnp.tanh(wa_in), wa_in)
    wa = _bdot(wa_in, wa_ref[...])
    gd = ps[:, c3 + DECAY_LORA + AAA_LORA:]
    g = _bdot(jax.nn.sigmoid(gd), g2_ref[...])

    wl = w0_ref[...] + wa[:, :width]
    softplus_neg = jnp.maximum(-wl, 0.0) + jnp.log1p(jnp.exp(-jnp.abs(wl)))
    logw = -jnp.exp(-softplus_neg - 0.5)
    a = jax.nn.sigmoid(a0_ref[...] + wa[:, width:])

    hr = lax.broadcasted_iota(jnp.int32, (width, width), 0) // RW_HEAD
    hc = lax.broadcasted_iota(jnp.int32, (width, width), 1) // RW_HEAD
    head_ones = jnp.where(hr == hc, 1.0, 0.0).astype(BF16)
    head_sum = lambda t: jnp.dot(t.astype(BF16), head_ones, preferred_element_type=F32)

    kk = k * kk_ref[...]
    kk = kk / jnp.maximum(jnp.sqrt(head_sum(kk * kk)), 1e-12)
    k2 = k * (1.0 + (a - 1.0) * ka_ref[...])
    bonus = head_sum(r * k2 * rk_ref[...]) * v

    tr = lax.broadcasted_iota(jnp.int32, (tb, tb), 0)
    tc = lax.broadcasted_iota(jnp.int32, (tb, tb), 1)
    chunk_tril = jnp.where((tr // C == tc // C) & (tc <= tr), 1.0, 0.0)
    cum = _split3_dot(chunk_tril, logw)
    g_inc = jnp.exp(cum)
    g_inv = jnp.exp(-cum)
    g_exc = jnp.exp(cum - logw)
    rt_ref[...] = r * g_inc
    kh_ref[...] = k2 * g_inv
    at_ref[...] = -kk * g_exc
    bh_ref[...] = kk * a * g_inv
    v_ref[...] = v
    gl_ref[...] = g_inc

    lane_p = lax.broadcasted_iota(jnp.int32, (C, RW_PAIR), 1)
    first = lane_p < RW_HEAD
    ri = lax.broadcasted_iota(jnp.int32, (RW_PAIR, RW_PAIR), 0)
    ci = lax.broadcasted_iota(jnp.int32, (RW_PAIR, RW_PAIR), 1)
    strict = ci < ri
    incl = ci <= ri
    eye = jnp.where(ci == ri, 1.0, 0.0)

    def stack(t):
        return jnp.concatenate([jnp.where(first, t, 0.0), jnp.where(first, 0.0, t)], axis=0)

    n_chunks = tb // C
    units = [(c, j) for c in range(n_chunks) for j in range(n_pairs)]
    U = range(len(units))
    rows = [slice(c * C, (c + 1) * C) for c, _ in units]
    lanes = [slice(j * RW_PAIR, (j + 1) * RW_PAIR) for _, j in units]
    rts = [stack(rt_ref[rows[n], lanes[n]]) for n in U]
    khs = [stack(kh_ref[rows[n], lanes[n]]) for n in U]
    ats = [stack(at_ref[rows[n], lanes[n]]) for n in U]
    bhs = [stack(bh_ref[rows[n], lanes[n]]) for n in U]
    vs = [stack(v_ref[rows[n], lanes[n]]) for n in U]
    g_last = [gl_ref[(c + 1) * C - 1:(c + 1) * C, lanes[n]] for n, (c, _) in enumerate(units)]

    sc = [_bdot_nt(jnp.concatenate([ats[n], rts[n]], axis=0), jnp.concatenate([bhs[n], khs[n]], axis=0))
          for n in U]
    l_ab = [jnp.where(strict, sc[n][:RW_PAIR, :RW_PAIR], 0.0) for n in U]
    l_ak = [jnp.where(strict, sc[n][:RW_PAIR, RW_PAIR:], 0.0) for n in U]
    a_rb = [jnp.where(incl, sc[n][RW_PAIR:, :RW_PAIR], 0.0) for n in U]
    a_rk = [jnp.where(incl, sc[n][RW_PAIR:, RW_PAIR:], 0.0) for n in U]

    pw = l_ab
    tinv = [eye + l_ab[n] for n in U]
    for _ in range(RW_INV_DOUBLINGS):
        pw = [_bdot(pw[n], pw[n]) for n in U]
        tinv = [tinv[n] + _bdot(tinv[n], pw[n]) for n in U]

    lv = [_bdot(l_ak[n], vs[n]) for n in U]
    g0 = [_bdot_tn(vs[n], khs[n]) for n in U]
    taw = [_bdot(tinv[n], jnp.concatenate([ats[n], lv[n]], axis=1)) for n in U]
    ar = [_bdot(jnp.concatenate([a_rb[n], a_rk[n]], axis=1),
                jnp.concatenate([taw[n], jnp.concatenate([jnp.zeros_like(vs[n]), vs[n]], axis=1)], axis=0))
          for n in U]
    qt_ta = [jnp.concatenate([rts[n] + ar[n][:, :RW_PAIR], taw[n][:, :RW_PAIR]], axis=0) for n in U]

    state = [s_ref[j] for j in range(n_pairs)]
    for c in range(n_chunks):
        ns = [c * n_pairs + j for j in range(n_pairs)]
        m1 = [_bdot_nt(qt_ta[n], state[j]) for j, n in enumerate(ns)]
        u = [m1[j][RW_PAIR:] + taw[n][:, RW_PAIR:] for j, n in enumerate(ns)]
        state = [(state[j] + g0[n] + _bdot_tn(u[j], bhs[n])) * g_last[n] for j, n in enumerate(ns)]
        for j, n in enumerate(ns):
            y = m1[j][:RW_PAIR] + ar[n][:, RW_PAIR:]
            y_ref[rows[n], lanes[n]] = y[:C] + y[C:]
    for j in range(n_pairs):
        s_ref[j] = state[j]

    y = y_ref[...]
    inv_n = 1.0 / RW_HEAD
    mean = head_sum(y) * inv_n
    yc = y - mean
    var = head_sum(yc * yc) * inv_n
    yn = yc * lax.rsqrt(var + GN_EPS) * lnw_ref[...] + lnb_ref[...]
    o_ref[0] = ((yn + bonus) * g).astype(o_ref.dtype)


def _rwkv7(p_rw, shift_mu, w0, w2, a0, a2, g2, k_k, k_a, r_k, lnx_w, lnx_b, B, S):
    cols = p_rw.shape[-1]
    width = (cols - DECAY_LORA - AAA_LORA - GATE_LORA) // 3
    tb = _pick_tile(S, 256)
    assert tb % RW_CHUNK == 0 and width % RW_PAIR == 0
    p3 = p_rw.reshape(B, S, cols)
    vec = lambda t: t.reshape(1, -1).astype(F32)
    zeros = jnp.zeros_like(w2)
    wa = jnp.concatenate([jnp.concatenate([w2, zeros], axis=1),
                          jnp.concatenate([jnp.zeros_like(a2), a2], axis=1)], axis=0).astype(BF16)
    blk = lambda b, i: (b, i, 0)
    prev_blk = lambda b, i: (b, jnp.maximum(i * (tb // 8) - 1, 0), 0)
    fixed = lambda b, i: (0, 0)
    wide = pltpu.VMEM((tb, width), F32)
    out = pl.pallas_call(
        functools.partial(_rwkv_body, tb=tb, width=width),
        grid=(B, S // tb),
        in_specs=[pl.BlockSpec((1, tb, cols), blk),
                  pl.BlockSpec((1, 8, cols), prev_blk),
                  pl.BlockSpec((1, cols), fixed)]
                 + [pl.BlockSpec((1, width), fixed)] * 7
                 + [pl.BlockSpec((DECAY_LORA + AAA_LORA, 2 * width), fixed),
                    pl.BlockSpec((GATE_LORA, width), fixed)],
        out_specs=pl.BlockSpec((1, tb, width), blk),
        out_shape=jax.ShapeDtypeStruct((B, S, width), BF16),
        scratch_shapes=[pltpu.VMEM((width // RW_PAIR, RW_PAIR, RW_PAIR), F32)] + [wide] * 7,
        compiler_params=_cparams("parallel", "arbitrary"),
        name="rwkv7_mix",
    )(p3, p3, vec(shift_mu), vec(w0), vec(a0), vec(k_k), vec(k_a), vec(r_k), vec(lnx_w), vec(lnx_b),
      wa, g2.astype(BF16))
    return out.reshape(B * S, width)


def _outproj_router_body(oda_ref, orw_ref, x_ref, wa_ref, wb_ref, nw_ref, rw_ref, rb_ref,
                         x1_ref, h_ref, idx_ref, gate_ref, rank_ref, cnt_ref, carry_ref):
    i = pl.program_id(0)

    @pl.when(i == 0)
    def _():
        carry_ref[...] = jnp.zeros(carry_ref.shape, F32)

    x1 = (x_ref[...] + jnp.dot(oda_ref[...], wa_ref[...], preferred_element_type=F32)
          + jnp.dot(orw_ref[...], wb_ref[...], preferred_element_type=F32))
    x1_ref[...] = x1
    h = (x1 * lax.rsqrt(jnp.mean(x1 * x1, axis=-1, keepdims=True) + NORM_EPS)) * nw_ref[...]
    h_ref[...] = h
    h_hi = h.astype(BF16)
    h_lo = (h - h_hi.astype(F32)).astype(BF16)
    rw = rw_ref[...]
    rw_hi = rw.astype(BF16)
    rw_lo = (rw - rw_hi.astype(F32)).astype(BF16)
    logits = (jnp.dot(h_hi, rw_hi, preferred_element_type=F32) + jnp.dot(h_lo, rw_hi, preferred_element_type=F32)
              + jnp.dot(h_hi, rw_lo, preferred_element_type=F32) + rb_ref[...])
    tm, n_exp = logits.shape
    lane = lax.broadcasted_iota(jnp.int32, logits.shape, 1).astype(F32)
    out_lane = lax.broadcasted_iota(jnp.int32, (tm, TOP_K), 1)

    vals, idxs = [], []
    rest = logits
    for _ in range(TOP_K):
        m = jnp.max(rest, axis=-1, keepdims=True)
        sel = jnp.min(jnp.where(rest == m, lane, float(n_exp)), axis=-1, keepdims=True)
        vals.append(m)
        idxs.append(sel)
        rest = jnp.where(lane == sel, -jnp.inf, rest)
    exps = [jnp.exp(v - vals[0]) for v in vals]
    denom = exps[0]
    for e in exps[1:]:
        denom = denom + e

    hot = [jnp.where(lane == s, 1.0, 0.0) for s in idxs]
    hot_all = hot[0]
    for t in hot[1:]:
        hot_all = hot_all + t
    tr = lax.broadcasted_iota(jnp.int32, (tm, tm), 0)
    tc = lax.broadcasted_iota(jnp.int32, (tm, tm), 1)
    before = jnp.dot(jnp.where(tc < tr, 1.0, 0.0).astype(BF16), hot_all.astype(BF16),
                     preferred_element_type=F32) + carry_ref[...]

    idx_out = jnp.zeros((tm, TOP_K), F32)
    gate_out = jnp.zeros((tm, TOP_K), F32)
    rank_out = jnp.zeros((tm, TOP_K), F32)
    for kk in range(TOP_K):
        rank_k = jnp.sum(hot[kk] * before, axis=-1, keepdims=True)
        idx_out = jnp.where(out_lane == kk, idxs[kk], idx_out)
        gate_out = jnp.where(out_lane == kk, exps[kk] / denom, gate_out)
        rank_out = jnp.where(out_lane == kk, rank_k, rank_out)
    idx_ref[...] = idx_out.astype(jnp.int32)
    gate_ref[...] = gate_out
    rank_ref[...] = rank_out.astype(jnp.int32)
    carry_ref[...] += jnp.sum(hot_all, axis=0, keepdims=True)
    cnt_ref[...] = carry_ref[...]


def _outproj_router(o_da, o_rw, x2, w_out, ffn_norm_w, router_w, router_b):
    T, D = x2.shape
    n_exp = router_w.shape[1]
    tm = _pick_tile(T, 512)
    wa = w_out[:DA_WIDTH].astype(BF16)
    wb = w_out[DA_WIDTH:].astype(BF16)
    row = lambda i: (i, 0)
    fixed = lambda i: (0, 0)
    return pl.pallas_call(
        _outproj_router_body,
        grid=(T // tm,),
        in_specs=[pl.BlockSpec((tm, DA_WIDTH), row),
                  pl.BlockSpec((tm, o_rw.shape[1]), row),
                  pl.BlockSpec((tm, D), row),
                  pl.BlockSpec(wa.shape, fixed),
                  pl.BlockSpec(wb.shape, fixed),
                  pl.BlockSpec((1, D), fixed),
                  pl.BlockSpec((D, n_exp), fixed),
                  pl.BlockSpec((1, n_exp), fixed)],
        out_specs=[pl.BlockSpec((tm, D), row),
                   pl.BlockSpec((tm, D), row),
                   pl.BlockSpec((tm, TOP_K), row),
                   pl.BlockSpec((tm, TOP_K), row),
                   pl.BlockSpec((tm, TOP_K), row),
                   pl.BlockSpec((1, n_exp), fixed)],
        out_shape=[jax.ShapeDtypeStruct((T, D), F32),
                   jax.ShapeDtypeStruct((T, D), F32),
                   jax.ShapeDtypeStruct((T, TOP_K), jnp.int32),
                   jax.ShapeDtypeStruct((T, TOP_K), F32),
                   jax.ShapeDtypeStruct((T, TOP_K), jnp.int32),
                   jax.ShapeDtypeStruct((1, n_exp), F32)],
        scratch_shapes=[pltpu.VMEM((1, n_exp), F32)],
        compiler_params=_cparams("arbitrary"),
        name="outproj_router",
    )(o_da, o_rw, x2, wa, wb, ffn_norm_w.reshape(1, D), router_w, router_b.reshape(1, n_exp))


MOE_ROWS = 512
DISPATCH_TOKENS = 256
ISSUE_UNROLL = 8
MXU_TILE = 256
SUBLANES = 8
ZERO_RUN = MOE_ROWS + SUBLANES


def _dispatch_body(pad_ref, dest_ref, h_ref, xs_hbm, zero_ref, sem):
    i = pl.program_id(0)

    @pl.when(i == 0)
    def _():
        zero_ref[...] = jnp.zeros(zero_ref.shape, zero_ref.dtype)

        def fill(e, carry):
            start = pl.multiple_of(pad_ref[e], SUBLANES)
            cp = pltpu.make_async_copy(zero_ref, xs_hbm.at[pl.ds(start, ZERO_RUN)], sem)
            cp.start()
            cp.wait()
            return carry

        n_exp = pad_ref.shape[0] - 1
        lax.fori_loop(0, n_exp, fill, 0)

        def fill_tail(b, carry):
            start = pl.multiple_of(b * MOE_ROWS, MOE_ROWS)
            cp = pltpu.make_async_copy(zero_ref.at[pl.ds(0, MOE_ROWS)], xs_hbm.at[pl.ds(start, MOE_ROWS)], sem)
            cp.start()
            cp.wait()
            return carry

        lax.fori_loop(pad_ref[n_exp], xs_hbm.shape[0] // MOE_ROWS, fill_tail, 0)

    def issue(t, carry):
        src = h_ref.at[pl.ds(t, 1)]
        for kk in range(TOP_K):
            pltpu.make_async_copy(src, xs_hbm.at[pl.ds(dest_ref[0, 0, t * TOP_K + kk], 1)],
                                  sem).start(priority=kk % 2)
        return carry

    lax.fori_loop(0, DISPATCH_TOKENS, issue, 0, unroll=ISSUE_UNROLL)
    for _ in range(TOP_K):
        pltpu.make_async_copy(h_ref, xs_hbm.at[pl.ds(0, DISPATCH_TOKENS)], sem).wait()


def _dispatch(h, dest, pad_start, n_padded):
    T, D = h.shape
    n_blocks = T // DISPATCH_TOKENS
    dest3 = dest.reshape(n_blocks, 1, DISPATCH_TOKENS * TOP_K)
    return pl.pallas_call(
        _dispatch_body,
        grid_spec=pltpu.PrefetchScalarGridSpec(
            num_scalar_prefetch=1,
            grid=(n_blocks,),
            in_specs=[pl.BlockSpec((1, 1, DISPATCH_TOKENS * TOP_K), lambda i, pad: (i, 0, 0),
                                   memory_space=pltpu.SMEM),
                      pl.BlockSpec((DISPATCH_TOKENS, D), lambda i, pad: (i, 0))],
            out_specs=pl.BlockSpec(memory_space=pl.ANY),
            scratch_shapes=[pltpu.VMEM((ZERO_RUN, D), h.dtype), pltpu.SemaphoreType.DMA(())]),
        out_shape=jax.ShapeDtypeStruct((n_padded, D), h.dtype),
        compiler_params=_cparams("arbitrary"),
        name="moe_dispatch",
    )(pad_start, dest3, h)


def _mlp1_regroup_body(w_ref, o_ref):
    half = MXU_TILE // 2
    r = lax.broadcasted_iota(jnp.int32, (MXU_TILE, MXU_TILE), 0)
    c = lax.broadcasted_iota(jnp.int32, (MXU_TILE, MXU_TILE), 1)
    perm = jnp.where(c == (r >> 1) + (r & 1) * half, 1.0, 0.0).astype(BF16)
    for g in range(w_ref.shape[2] // MXU_TILE):
        sl = slice(g * MXU_TILE, (g + 1) * MXU_TILE)
        o_ref[0, :, sl] = jnp.dot(w_ref[0, :, sl].astype(BF16), perm,
                                  preferred_element_type=F32).astype(o_ref.dtype)


def _mlp1_regroup(mlp1_w):
    n_exp, D, cols = mlp1_w.shape
    tc = _pick_tile(cols, 2 * MXU_TILE)
    blk = lambda e, j: (e, 0, j)
    return pl.pallas_call(
        _mlp1_regroup_body,
        grid=(n_exp, cols // tc),
        in_specs=[pl.BlockSpec((1, D, tc), blk)],
        out_specs=pl.BlockSpec((1, D, tc), blk),
        out_shape=jax.ShapeDtypeStruct(mlp1_w.shape, BF16),
        compiler_params=_cparams("parallel", "parallel"),
        name="mlp1_regroup",
    )(mlp1_w)


def _experts_body(be_ref, nu_ref, xs_ref, w1_ref, b1_ref, w2_ref, b2_ref, y_ref):
    del be_ref
    i = pl.program_id(0)
    half = MXU_TILE // 2

    @pl.when(i < nu_ref[0])
    def _():
        x = xs_ref[...].astype(BF16)
        gu = jnp.dot(x, w1_ref[0], preferred_element_type=F32) + b1_ref[0]
        acts = []
        for g in range(gu.shape[1] // MXU_TILE):
            gate = jnp.minimum(gu[:, g * MXU_TILE:g * MXU_TILE + half], SWIGLU_LIMIT)
            up = jnp.clip(gu[:, g * MXU_TILE + half:(g + 1) * MXU_TILE], -SWIGLU_LIMIT, SWIGLU_LIMIT)
            acts.append(((up + 1.0) * gate * jax.nn.sigmoid(SWIGLU_ALPHA * gate)).astype(BF16))
        act = jnp.concatenate(acts, axis=1)
        y_ref[...] = jnp.dot(act, w2_ref[0], preferred_element_type=F32) + b2_ref[0]

    @pl.when(i >= nu_ref[0])
    def _():
        y_ref[...] = jnp.zeros(y_ref.shape, y_ref.dtype)


def _experts(xs, blk_expert, n_used, w1, b1, w2, b2):
    n_padded, D = xs.shape
    n_blocks = n_padded // MOE_ROWS
    gu_cols = w1.shape[2]
    xrow = lambda i, be, nu: (jnp.maximum(jnp.minimum(i, nu[0] - 1), 0), 0)
    yrow = lambda i, be, nu: (i, 0)
    wsel = lambda i, be, nu: (be[i], 0, 0)
    return pl.pallas_call(
        _experts_body,
        grid_spec=pltpu.PrefetchScalarGridSpec(
            num_scalar_prefetch=2,
            grid=(n_blocks,),
            in_specs=[pl.BlockSpec((MOE_ROWS, D), xrow),
                      pl.BlockSpec((1, D, gu_cols), wsel),
                      pl.BlockSpec((1, 1, gu_cols), wsel),
                      pl.BlockSpec((1, gu_cols // 2, D), wsel),
                      pl.BlockSpec((1, 1, D), wsel)],
            out_specs=pl.BlockSpec((MOE_ROWS, D), yrow)),
        out_shape=jax.ShapeDtypeStruct((n_padded, D), F32),
        compiler_params=_cparams("arbitrary"),
        name="moe_experts",
    )(blk_expert, n_used, xs, w1, b1, w2, b2)


def _combine_body(dest_ref, next_ref, ys_hbm, x1_ref, gate_ref, fw_ref, o_ref, buf_ref, sems):
    i = pl.program_id(0)
    slot = i % 2

    def gather(d_ref, s):
        def issue(t, carry):
            for kk in range(TOP_K):
                pltpu.make_async_copy(ys_hbm.at[pl.ds(d_ref[0, 0, t * TOP_K + kk], 1)],
                                      buf_ref.at[s, kk, pl.ds(t, 1)], sems.at[s]).start(priority=kk % 2)
            return carry

        lax.fori_loop(0, DISPATCH_TOKENS, issue, 0, unroll=ISSUE_UNROLL)

    @pl.when(i == 0)
    def _():
        gather(dest_ref, 0)

    @pl.when(i + 1 < pl.num_programs(0))
    def _():
        gather(next_ref, 1 - slot)

    for kk in range(TOP_K):
        pltpu.make_async_copy(ys_hbm.at[pl.ds(0, DISPATCH_TOKENS)], buf_ref.at[slot, kk], sems.at[slot]).wait()

    gates = gate_ref[...]
    x2 = x1_ref[...]
    for kk in range(TOP_K):
        x2 = x2 + gates[:, kk:kk + 1] * buf_ref[slot, kk]
    o_ref[...] = (x2 * lax.rsqrt(jnp.mean(x2 * x2, axis=-1, keepdims=True) + NORM_EPS)) * fw_ref[...]


def _combine(ys, dest, x1, gates, final_w):
    T, D = x1.shape
    n_blocks = T // DISPATCH_TOKENS
    dest3 = dest.reshape(n_blocks, 1, DISPATCH_TOKENS * TOP_K)
    row = lambda i: (i, 0)
    idx_blk = (1, 1, DISPATCH_TOKENS * TOP_K)
    return pl.pallas_call(
        _combine_body,
        grid=(n_blocks,),
        in_specs=[pl.BlockSpec(idx_blk, lambda i: (i, 0, 0), memory_space=pltpu.SMEM),
                  pl.BlockSpec(idx_blk, lambda i: (jnp.minimum(i + 1, n_blocks - 1), 0, 0),
                               memory_space=pltpu.SMEM),
                  pl.BlockSpec(memory_space=pl.ANY),
                  pl.BlockSpec((DISPATCH_TOKENS, D), row),
                  pl.BlockSpec((DISPATCH_TOKENS, TOP_K), row),
                  pl.BlockSpec((1, D), lambda i: (0, 0))],
        out_specs=pl.BlockSpec((DISPATCH_TOKENS, D), row),
        out_shape=jax.ShapeDtypeStruct((T, D), F32),
        scratch_shapes=[pltpu.VMEM((2, TOP_K, DISPATCH_TOKENS, D), F32), pltpu.SemaphoreType.DMA((2,))],
        compiler_params=_cparams("arbitrary"),
        name="moe_combine",
    )(dest3, dest3, ys, x1, gates, final_w.reshape(1, D))


def _moe(h, x1, idx, gates, rank, counts, mlp1_w, mlp1_b, mlp2_w, mlp2_b, final_w):
    T, D = h.shape
    n_exp = mlp1_w.shape[0]
    assert T % DISPATCH_TOKENS == 0
    n_padded = T * TOP_K + n_exp * MOE_ROWS
    n_blocks = n_padded // MOE_ROWS
    counts = counts.reshape(n_exp).astype(jnp.int32)
    padded = (counts + MOE_ROWS - 1) // MOE_ROWS * MOE_ROWS
    pend = jnp.cumsum(padded)
    pstart = pend - padded
    dest = (pstart[idx] + rank).astype(jnp.int32)
    starts = jnp.arange(n_blocks, dtype=jnp.int32) * MOE_ROWS
    blk_expert = jnp.minimum(jnp.sum((pend[None, :] <= starts[:, None]).astype(jnp.int32), axis=1),
                             n_exp - 1).astype(jnp.int32)
    n_used = (pend[-1:] // MOE_ROWS).astype(jnp.int32)

    w1 = _mlp1_regroup(mlp1_w)
    half = MXU_TILE // 2
    b1 = mlp1_b.reshape(n_exp, -1, half, 2).transpose(0, 1, 3, 2).reshape(n_exp, 1, -1)
    pad_start = jnp.minimum((pstart + counts) // SUBLANES * SUBLANES, n_padded - ZERO_RUN).astype(jnp.int32)
    xs = _dispatch(h, dest, jnp.concatenate([pad_start, n_used]), n_padded)
    ys = _experts(xs, blk_expert, n_used, w1, b1, mlp2_w.astype(BF16), mlp2_b[:, None, :])
    return _combine(ys, dest, x1, gates, final_w)


def kernel(x, positions, attn_norm_w, w_in, shift_mu, lambda_q1, lambda_k1, lambda_q2, lambda_k2, subln_w,
           rw_w0, rw_w2, rw_a0, rw_a2, rw_g2, rw_k_k, rw_k_a, rw_r_k, rw_lnx_w, rw_lnx_b, w_out, ffn_norm_w,
           router_w, router_b, mlp1_w, mlp1_b, mlp2_w, mlp2_b, final_norm_w):
    B, S, D = x.shape
    assert attn_norm_w.shape[0] == 1, "single-layer stack"
    l = LAYER_INDEX
    x2 = x.reshape(B * S, D)
    cos_t, sin_t = _rope_tables(positions)
    q, k, v, p_rw = _input_projection(x2, attn_norm_w[l], w_in[l], cos_t, sin_t)
    o_da = _diff_attention(q, k, v, lambda_q1[l], lambda_k1[l], lambda_q2[l], lambda_k2[l], subln_w[l], B, S)
    o_rw = _rwkv7(p_rw, shift_mu[l], rw_w0[l], rw_w2[l], rw_a0[l], rw_a2[l], rw_g2[l],
                  rw_k_k[l], rw_k_a[l], rw_r_k[l], rw_lnx_w[l], rw_lnx_b[l], B, S)
    x1, h, idx, gates, rank, counts = _outproj_router(o_da, o_rw, x2, w_out[l], ffn_norm_w[l],
                                                      router_w[l], router_b[l])
    out = _moe(h, x1, idx, gates, rank, counts, mlp1_w[l], mlp1_b[l], mlp2_w[l], mlp2_b[l], final_norm_w)
    return out.reshape(B, S, D)
```

```python
import functools
import math

import jax
import jax.numpy as jnp
from jax import lax
from jax.experimental import pallas as pl
from jax.experimental.pallas import tpu as pltpu

F32 = jnp.float32
BF16 = jnp.bfloat16

V7X_LANES = 128
V7X_VMEM_BYTES = 64 * 1024 * 1024
VMEM_LIMIT_BYTES = 56 * 1024 * 1024

DA_HEADS = 4
DA_HEAD_DIM = 64
DA_V_DIM = 2 * DA_HEAD_DIM
DA_WIDTH = DA_HEADS * DA_V_DIM
RW_HEAD = 64
DECAY_LORA = 64
AAA_LORA = 64
GATE_LORA = 128
ROPE_THETA = 10000.0
N_EXPERTS = 32
TOP_K = 4
SWIGLU_ALPHA = 1.702
SWIGLU_LIMIT = 7.0
NORM_EPS = 1e-5
GN_EPS = 64e-5
LAYER_INDEX = 0
LAM_INIT = 0.8 - 0.6 * math.exp(-0.3 * LAYER_INDEX)


def _cparams(*semantics):
    return pltpu.CompilerParams(dimension_semantics=semantics, vmem_limit_bytes=VMEM_LIMIT_BYTES)


def _pick_tile(n, want):
    t = min(n, want)
    while n % t:
        t //= 2
    return t


def _inproj_body(x_ref, nw_ref, w_ref, cos_ref, sin_ref, q_ref, k_ref, v_ref, prw_ref):
    x = x_ref[...]
    ms = jnp.mean(x * x, axis=-1, keepdims=True)
    h = (x * lax.rsqrt(ms + NORM_EPS)) * nw_ref[...]
    p = jnp.dot(h.astype(BF16), w_ref[...], preferred_element_type=F32)
    cos = cos_ref[...]
    sin = sin_ref[...]
    lane = lax.broadcasted_iota(jnp.int32, cos.shape, 1)
    first_half = (lane % DA_HEAD_DIM) < (DA_HEAD_DIM // 2)
    scale = DA_HEAD_DIM ** -0.5

    def rope(t):
        partner = jnp.where(first_half,
                            pltpu.roll(t, V7X_LANES - DA_HEAD_DIM // 2, 1),
                            pltpu.roll(t, DA_HEAD_DIM // 2, 1))
        return t * cos + partner * sin

    for hh in range(DA_HEADS):
        sl = slice(hh * DA_V_DIM, (hh + 1) * DA_V_DIM)
        q_ref[:, sl] = (rope(p[:, sl]) * scale).astype(q_ref.dtype)
        ksl = slice(DA_WIDTH + hh * DA_V_DIM, DA_WIDTH + (hh + 1) * DA_V_DIM)
        k_ref[:, sl] = rope(p[:, ksl]).astype(k_ref.dtype)
    v_ref[...] = p[:, 2 * DA_WIDTH:3 * DA_WIDTH].astype(v_ref.dtype)
    prw_ref[...] = p[:, 3 * DA_WIDTH:]


def _input_projection(x2, norm_w, w_in, cos_t, sin_t):
    T, D = x2.shape
    n_cols = w_in.shape[1]
    rw_cols = n_cols - 3 * DA_WIDTH
    tm = _pick_tile(T, 512)
    row = lambda i: (i, 0)
    fixed = lambda i: (0, 0)
    return pl.pallas_call(
        _inproj_body,
        grid=(T // tm,),
        in_specs=[pl.BlockSpec((tm, D), row),
                  pl.BlockSpec((1, D), fixed),
                  pl.BlockSpec((D, n_cols), fixed),
                  pl.BlockSpec((tm, DA_V_DIM), row),
                  pl.BlockSpec((tm, DA_V_DIM), row)],
        out_specs=[pl.BlockSpec((tm, DA_WIDTH), row),
                   pl.BlockSpec((tm, DA_WIDTH), row),
                   pl.BlockSpec((tm, DA_WIDTH), row),
                   pl.BlockSpec((tm, rw_cols), row)],
        out_shape=[jax.ShapeDtypeStruct((T, DA_WIDTH), BF16),
                   jax.ShapeDtypeStruct((T, DA_WIDTH), BF16),
                   jax.ShapeDtypeStruct((T, DA_WIDTH), BF16),
                   jax.ShapeDtypeStruct((T, rw_cols), F32)],
        compiler_params=_cparams("parallel"),
        name="input_projection",
    )(x2, norm_w.reshape(1, D), w_in.astype(BF16), cos_t, sin_t)


def _rope_tables(positions):
    half = DA_HEAD_DIM // 2
    inv = ROPE_THETA ** (-jnp.arange(0, DA_HEAD_DIM, 2, dtype=F32) / DA_HEAD_DIM)
    ang = positions.reshape(-1).astype(F32)[:, None] * inv
    cos, sin = jnp.cos(ang), jnp.sin(ang)
    cos_t = jnp.tile(cos, (1, DA_V_DIM // half))
    sin_t = jnp.tile(jnp.concatenate([-sin, sin], axis=-1), (1, DA_V_DIM // DA_HEAD_DIM))
    return cos_t, sin_t


ATTN_Q_BLOCK = 1024
ATTN_K_BLOCK = 512
ATTN_Q_SUB = 128


def _diff_attn_body(q_ref, k_ref, v_ref, lq1_ref, lk1_ref, lq2_ref, lk2_ref, sw_ref, o_ref,
                    m_ref, acc_ref, *, tq, tk, q_sub):
    qi = pl.program_id(2)
    q = q_ref[0]
    lane = lax.broadcasted_iota(jnp.int32, q.shape, 1)
    zero = jnp.zeros_like(q)
    q_heads = (jnp.where(lane < DA_HEAD_DIM, q, zero), jnp.where(lane >= DA_HEAD_DIM, q, zero))

    m_ref[...] = jnp.full(m_ref.shape, -jnp.inf, F32)
    acc_ref[...] = jnp.zeros(acc_ref.shape, F32)

    def step(j, diag):
        ks = pl.multiple_of(j * tk, tk)
        kb = k_ref[0, pl.ds(ks, tk), :]
        vb = v_ref[0, pl.ds(ks, tk), :]
        v_ext = jnp.concatenate([vb, jnp.ones_like(vb)], axis=1)
        k0 = 0 if diag is None else diag * tk
        units = []
        for r0 in range(0, tq, q_sub):
            n_keys, masked = tk, False
            if diag is not None:
                n_keys = min(tk, r0 + q_sub - k0)
                if n_keys <= 0:
                    continue
                masked = k0 + n_keys - 1 > r0
            units += [(c, r0, n_keys, masked) for c in range(2)]
        scores = [lax.dot_general(q_heads[c][r0:r0 + q_sub], kb[:n_keys], (((1,), (1,)), ((), ())),
                                  preferred_element_type=F32) for c, r0, n_keys, _ in units]
        probs, alphas = [], []
        for (c, r0, n_keys, masked), s in zip(units, scores):
            if masked:
                ri = lax.broadcasted_iota(jnp.int32, s.shape, 0) + r0
                ci = lax.broadcasted_iota(jnp.int32, s.shape, 1) + k0
                s = jnp.where(ci <= ri, s, -jnp.inf)
            m_prev = m_ref[c, r0:r0 + q_sub, :]
            m_new = jnp.maximum(m_prev, jnp.max(s, axis=-1, keepdims=True))
            m_ref[c, r0:r0 + q_sub, :] = m_new
            alphas.append(jnp.exp(m_prev - m_new))
            probs.append(jnp.concatenate(
                [jnp.exp(s[:, t * V7X_LANES:(t + 1) * V7X_LANES] - m_new).astype(vb.dtype)
                 for t in range(n_keys // V7X_LANES)], axis=1))
        for (c, r0, n_keys, _), p, alpha in zip(units, probs, alphas):
            pv = jnp.dot(p, v_ext[:n_keys], preferred_element_type=F32)
            rows = slice(r0, r0 + q_sub)
            acc_ref[c, rows, :] = jnp.concatenate([alpha, alpha], axis=1) * acc_ref[c, rows, :] + pv

    def full_step(j, carry):
        step(j, None)
        return carry

    lax.fori_loop(0, qi * (tq // tk), full_step, 0)
    for d in range(tq // tk):
        step(qi * (tq // tk) + d, d)

    lam = (jnp.exp(jnp.sum(lq1_ref[...] * lk1_ref[...], axis=-1, keepdims=True))
           - jnp.exp(jnp.sum(lq2_ref[...] * lk2_ref[...], axis=-1, keepdims=True)) + LAM_INIT)
    a0, a1 = acc_ref[0], acc_ref[1]
    o = a0[:, :DA_V_DIM] / a0[:, DA_V_DIM:] - lam * (a1[:, :DA_V_DIM] / a1[:, DA_V_DIM:])
    o = o * lax.rsqrt(jnp.mean(o * o, axis=-1, keepdims=True) + NORM_EPS)
    o_ref[0] = (o * sw_ref[...] * (1.0 - LAM_INIT)).astype(o_ref.dtype)


def _diff_attention(q, k, v, lq1, lk1, lq2, lk2, subln_w, B, S):
    tq = _pick_tile(S, ATTN_Q_BLOCK)
    tk = _pick_tile(tq, ATTN_K_BLOCK)
    q_sub = _pick_tile(tk, ATTN_Q_SUB)
    q3, k3, v3 = (t.reshape(B, S, DA_WIDTH) for t in (q, k, v))
    vec = lambda a: a.reshape(1, -1).astype(F32)
    blk = lambda b, h, i: (b, i, h)
    seq = lambda b, h, i: (b, 0, h)
    fixed = lambda b, h, i: (0, 0)
    out = pl.pallas_call(
        functools.partial(_diff_attn_body, tq=tq, tk=tk, q_sub=q_sub),
        grid=(B, DA_HEADS, S // tq),
        in_specs=[pl.BlockSpec((1, tq, DA_V_DIM), blk),
                  pl.BlockSpec((1, S, DA_V_DIM), seq),
                  pl.BlockSpec((1, S, DA_V_DIM), seq),
                  pl.BlockSpec((1, DA_HEAD_DIM), fixed),
                  pl.BlockSpec((1, DA_HEAD_DIM), fixed),
                  pl.BlockSpec((1, DA_HEAD_DIM), fixed),
                  pl.BlockSpec((1, DA_HEAD_DIM), fixed),
                  pl.BlockSpec((1, DA_V_DIM), fixed)],
        out_specs=pl.BlockSpec((1, tq, DA_V_DIM), blk),
        out_shape=jax.ShapeDtypeStruct((B, S, DA_WIDTH), BF16),
        scratch_shapes=[pltpu.VMEM((2, tq, V7X_LANES), F32),
                        pltpu.VMEM((2, tq, 2 * DA_V_DIM), F32)],
        compiler_params=_cparams("parallel", "parallel", "arbitrary"),
        name="diff_attention",
    )(q3, k3, v3, vec(lq1), vec(lk1), vec(lq2), vec(lk2), vec(subln_w))
    return out.reshape(B * S, DA_WIDTH)


RW_CHUNK = 64
RW_PAIR = 2 * RW_HEAD
RW_INV_DOUBLINGS = 5


def _bdot(a, b):
    return jnp.dot(a.astype(BF16), b.astype(BF16), preferred_element_type=F32)


def _bdot_nt(a, b):
    return lax.dot_general(a.astype(BF16), b.astype(BF16), (((1,), (1,)), ((), ())),
                           preferred_element_type=F32)


def _bdot_tn(a, b):
    return lax.dot_general(a.astype(BF16), b.astype(BF16), (((0,), (0,)), ((), ())),
                           preferred_element_type=F32)


def _split3_dot(m, x):
    mb = m.astype(BF16)
    x0 = x.astype(BF16)
    r1 = x - x0.astype(F32)
    x1 = r1.astype(BF16)
    x2 = (r1 - x1.astype(F32)).astype(BF16)
    dot = lambda t: jnp.dot(mb, t, preferred_element_type=F32)
    return dot(x0) + dot(x1) + dot(x2)


def _rwkv_body(p_ref, prev_ref, mu_ref, w0_ref, a0_ref, kk_ref, ka_ref, rk_ref, lnw_ref, lnb_ref,
               wa_ref, g2_ref, o_ref,
               s_ref, rt_ref, kh_ref, at_ref, bh_ref, v_ref, gl_ref, y_ref, *, tb, width):
    i = pl.program_id(1)
    n_pairs = width // RW_PAIR
    C = RW_CHUNK

    @pl.when(i == 0)
    def _():
        s_ref[...] = jnp.zeros(s_ref.shape, F32)

    p = p_ref[0]
    row = lax.broadcasted_iota(jnp.int32, p.shape, 0)
    last_prev = jnp.where(i == 0, 0.0, prev_ref[0, 7:8, :])
    prev = jnp.where(row == 0, last_prev, pltpu.roll(p, 1, 0))
    ps = p + mu_ref[...] * (prev - p)
    r = ps[:, 0:width]
    k = ps[:, width:2 * width]
    v = ps[:, 2 * width:3 * width]
    c3 = 3 * width
    wa_in = ps[:, c3:c3 + DECAY_LORA + AAA_LORA]
    lane = lax.broadcasted_iota(jnp.int32, wa_in.shape, 1)
    wa_in = jnp.where(lane < DECAY_LORA, jnp.tanh(wa_in), wa_in)
    wa = _bdot(wa_in, wa_ref[...])
    gd = ps[:, c3 + DECAY_LORA + AAA_LORA:]
    g = _bdot(jax.nn.sigmoid(gd), g2_ref[...])

    wl = w0_ref[...] + wa[:, :width]
    softplus_neg = jnp.maximum(-wl, 0.0) + jnp.log1p(jnp.exp(-jnp.abs(wl)))
    logw = -jnp.exp(-softplus_neg - 0.5)
    a = jax.nn.sigmoid(a0_ref[...] + wa[:, width:])

    hr = lax.broadcasted_iota(jnp.int32, (width, width), 0) // RW_HEAD
    hc = lax.broadcasted_iota(jnp.int32, (width, width), 1) // RW_HEAD
    head_ones = jnp.where(hr == hc, 1.0, 0.0).astype(BF16)
    head_sum = lambda t: jnp.dot(t.astype(BF16), head_ones, preferred_element_type=F32)

    kk = k * kk_ref[...]
    kk = kk / jnp.maximum(jnp.sqrt(head_sum(kk * kk)), 1e-12)
    k2 = k * (1.0 + (a - 1.0) * ka_ref[...])
    bonus = head_sum(r * k2 * rk_ref[...]) * v

    tr = lax.broadcasted_iota(jnp.int32, (tb, tb), 0)
    tc = lax.broadcasted_iota(jnp.int32, (tb, tb), 1)
    chunk_tril = jnp.where((tr // C == tc // C) & (tc <= tr), 1.0, 0.0)
    cum = _split3_dot(chunk_tril, logw)
    g_inc = jnp.exp(cum)
    g_inv = jnp.exp(-cum)
    g_exc = jnp.exp(cum - logw)
    rt_ref[...] = r * g_inc
    kh_ref[...] = k2 * g_inv
    at_ref[...] = -kk * g_exc
    bh_ref[...] = kk * a * g_inv
    v_ref[...] = v
    gl_ref[...] = g_inc

    lane_p = lax.broadcasted_iota(jnp.int32, (C, RW_PAIR), 1)
    first = lane_p < RW_HEAD
    ri = lax.broadcasted_iota(jnp.int32, (RW_PAIR, RW_PAIR), 0)
    ci = lax.broadcasted_iota(jnp.int32, (RW_PAIR, RW_PAIR), 1)
    strict = ci < ri
    incl = ci <= ri
    eye = jnp.where(ci == ri, 1.0, 0.0)

    def stack(t):
        return jnp.concatenate([jnp.where(first, t, 0.0), jnp.where(first, 0.0, t)], axis=0)

    n_chunks = tb // C
    units = [(c, j) for c in range(n_chunks) for j in range(n_pairs)]
    U = range(len(units))
    rows = [slice(c * C, (c + 1) * C) for c, _ in units]
    lanes = [slice(j * RW_PAIR, (j + 1) * RW_PAIR) for _, j in units]
    rts = [stack(rt_ref[rows[n], lanes[n]]) for n in U]
    khs = [stack(kh_ref[rows[n], lanes[n]]) for n in U]
    ats = [stack(at_ref[rows[n], lanes[n]]) for n in U]
    bhs = [stack(bh_ref[rows[n], lanes[n]]) for n in U]
    vs = [stack(v_ref[rows[n], lanes[n]]) for n in U]
    g_last = [gl_ref[(c + 1) * C - 1:(c + 1) * C, lanes[n]] for n, (c, _) in enumerate(units)]

    sc = [_bdot_nt(jnp.concatenate([ats[n], rts[n]], axis=0), jnp.concatenate([bhs[n], khs[n]], axis=0))
          for n in U]
    l_ab = [jnp.where(strict, sc[n][:RW_PAIR, :RW_PAIR], 0.0) for n in U]
    l_ak = [jnp.where(strict, sc[n][:RW_PAIR, RW_PAIR:], 0.0) for n in U]
    a_rb = [jnp.where(incl, sc[n][RW_PAIR:, :RW_PAIR], 0.0) for n in U]
    a_rk = [jnp.where(incl, sc[n][RW_PAIR:, RW_PAIR:], 0.0) for n in U]

    pw = l_ab
    tinv = [eye + l_ab[n] for n in U]
    for _ in range(RW_INV_DOUBLINGS):
        pw = [_bdot(pw[n], pw[n]) for n in U]
        tinv = [tinv[n] + _bdot(tinv[n], pw[n]) for n in U]

    lv = [_bdot(l_ak[n], vs[n]) for n in U]
    g0 = [_bdot_tn(vs[n], khs[n]) for n in U]
    taw = [_bdot(tinv[n], jnp.concatenate([ats[n], lv[n]], axis=1)) for n in U]
    ar = [_bdot(jnp.concatenate([a_rb[n], a_rk[n]], axis=1),
                jnp.concatenate([taw[n], jnp.concatenate([jnp.zeros_like(vs[n]), vs[n]], axis=1)], axis=0))
          for n in U]
    qt_ta = [jnp.concatenate([rts[n] + ar[n][:, :RW_PAIR], taw[n][:, :RW_PAIR]], axis=0) for n in U]

    state = [s_ref[j] for j in range(n_pairs)]
    for c in range(n_chunks):
        ns = [c * n_pairs + j for j in range(n_pairs)]
        m1 = [_bdot_nt(qt_ta[n], state[j]) for j, n in enumerate(ns)]
        u = [m1[j][RW_PAIR:] + taw[n][:, RW_PAIR:] for j, n in enumerate(ns)]
        state = [(state[j] + g0[n] + _bdot_tn(u[j], bhs[n])) * g_last[n] for j, n in enumerate(ns)]
        for j, n in enumerate(ns):
            y = m1[j][:RW_PAIR] + ar[n][:, RW_PAIR:]
            y_ref[rows[n], lanes[n]] = y[:C] + y[C:]
    for j in range(n_pairs):
        s_ref[j] = state[j]

    y = y_ref[...]
    inv_n = 1.0 / RW_HEAD
    mean = head_sum(y) * inv_n
    yc = y - mean
    var = head_sum(yc * yc) * inv_n
    yn = yc * lax.rsqrt(var + GN_EPS) * lnw_ref[...] + lnb_ref[...]
    o_ref[0] = ((yn + bonus) * g).astype(o_ref.dtype)


def _rwkv7(p_rw, shift_mu, w0, w2, a0, a2, g2, k_k, k_a, r_k, lnx_w, lnx_b, B, S):
    cols = p_rw.shape[-1]
    width = (cols - DECAY_LORA - AAA_LORA - GATE_LORA) // 3
    tb = _pick_tile(S, 256)
    assert tb % RW_CHUNK == 0 and width % RW_PAIR == 0
    p3 = p_rw.reshape(B, S, cols)
    vec = lambda t: t.reshape(1, -1).astype(F32)
    zeros = jnp.zeros_like(w2)
    wa = jnp.concatenate([jnp.concatenate([w2, zeros], axis=1),
                          jnp.concatenate([jnp.zeros_like(a2), a2], axis=1)], axis=0).astype(BF16)
    blk = lambda b, i: (b, i, 0)
    prev_blk = lambda b, i: (b, jnp.maximum(i * (tb // 8) - 1, 0), 0)
    fixed = lambda b, i: (0, 0)
    wide = pltpu.VMEM((tb, width), F32)
    out = pl.pallas_call(
        functools.partial(_rwkv_body, tb=tb, width=width),
        grid=(B, S // tb),
        in_specs=[pl.BlockSpec((1, tb, cols), blk),
                  pl.BlockSpec((1, 8, cols), prev_blk),
                  pl.BlockSpec((1, cols), fixed)]
                 + [pl.BlockSpec((1, width), fixed)] * 7
                 + [pl.BlockSpec((DECAY_LORA + AAA_LORA, 2 * width), fixed),
                    pl.BlockSpec((GATE_LORA, width), fixed)],
        out_specs=pl.BlockSpec((1, tb, width), blk),
        out_shape=jax.ShapeDtypeStruct((B, S, width), BF16),
        scratch_shapes=[pltpu.VMEM((width // RW_PAIR, RW_PAIR, RW_PAIR), F32)] + [wide] * 7,
        compiler_params=_cparams("parallel", "arbitrary"),
        name="rwkv7_mix",
    )(p3, p3, vec(shift_mu), vec(w0), vec(a0), vec(k_k), vec(k_a), vec(r_k), vec(lnx_w), vec(lnx_b),
      wa, g2.astype(BF16))
    return out.reshape(B * S, width)


def _outproj_router_body(oda_ref, orw_ref, x_ref, wa_ref, wb_ref, nw_ref, rw_ref, rb_ref,
                         x1_ref, h_ref, idx_ref, gate_ref, rank_ref, cnt_ref, carry_ref):
    i = pl.program_id(0)

    @pl.when(i == 0)
    def _():
        carry_ref[...] = jnp.zeros(carry_ref.shape, F32)

    x1 = (x_ref[...] + jnp.dot(oda_ref[...], wa_ref[...], preferred_element_type=F32)
          + jnp.dot(orw_ref[...], wb_ref[...], preferred_element_type=F32))
    x1_ref[...] = x1
    h = (x1 * lax.rsqrt(jnp.mean(x1 * x1, axis=-1, keepdims=True) + NORM_EPS)) * nw_ref[...]
    h_ref[...] = h
    h_hi = h.astype(BF16)
    h_lo = (h - h_hi.astype(F32)).astype(BF16)
    rw = rw_ref[...]
    rw_hi = rw.astype(BF16)
    rw_lo = (rw - rw_hi.astype(F32)).astype(BF16)
    logits = (jnp.dot(h_hi, rw_hi, preferred_element_type=F32) + jnp.dot(h_lo, rw_hi, preferred_element_type=F32)
              + jnp.dot(h_hi, rw_lo, preferred_element_type=F32) + rb_ref[...])
    tm, n_exp = logits.shape
    lane = lax.broadcasted_iota(jnp.int32, logits.shape, 1).astype(F32)
    out_lane = lax.broadcasted_iota(jnp.int32, (tm, TOP_K), 1)

    vals, idxs = [], []
    rest = logits
    for _ in range(TOP_K):
        m = jnp.max(rest, axis=-1, keepdims=True)
        sel = jnp.min(jnp.where(rest == m, lane, float(n_exp)), axis=-1, keepdims=True)
        vals.append(m)
        idxs.append(sel)
        rest = jnp.where(lane == sel, -jnp.inf, rest)
    exps = [jnp.exp(v - vals[0]) for v in vals]
    denom = exps[0]
    for e in exps[1:]:
        denom = denom + e

    hot = [jnp.where(lane == s, 1.0, 0.0) for s in idxs]
    hot_all = hot[0]
    for t in hot[1:]:
        hot_all = hot_all + t
    tr = lax.broadcasted_iota(jnp.int32, (tm, tm), 0)
    tc = lax.broadcasted_iota(jnp.int32, (tm, tm), 1)
    before = jnp.dot(jnp.where(tc < tr, 1.0, 0.0).astype(BF16), hot_all.astype(BF16),
                     preferred_element_type=F32) + carry_ref[...]

    idx_out = jnp.zeros((tm, TOP_K), F32)
    gate_out = jnp.zeros((tm, TOP_K), F32)
    rank_out = jnp.zeros((tm, TOP_K), F32)
    for kk in range(TOP_K):
        rank_k = jnp.sum(hot[kk] * before, axis=-1, keepdims=True)
        idx_out = jnp.where(out_lane == kk, idxs[kk], idx_out)
        gate_out = jnp.where(out_lane == kk, exps[kk] / denom, gate_out)
        rank_out = jnp.where(out_lane == kk, rank_k, rank_out)
    idx_ref[...] = idx_out.astype(jnp.int32)
    gate_ref[...] = gate_out
    rank_ref[...] = rank_out.astype(jnp.int32)
    carry_ref[...] += jnp.sum(hot_all, axis=0, keepdims=True)
    cnt_ref[...] = carry_ref[...]


def _outproj_router(o_da, o_rw, x2, w_out, ffn_norm_w, router_w, router_b):
    T, D = x2.shape
    n_exp = router_w.shape[1]
    tm = _pick_tile(T, 512)
    wa = w_out[:DA_WIDTH].astype(BF16)
    wb = w_out[DA_WIDTH:].astype(BF16)
    row = lambda i: (i, 0)
    fixed = lambda i: (0, 0)
    return pl.pallas_call(
        _outproj_router_body,
        grid=(T // tm,),
        in_specs=[pl.BlockSpec((tm, DA_WIDTH), row),
                  pl.BlockSpec((tm, o_rw.shape[1]), row),
                  pl.BlockSpec((tm, D), row),
                  pl.BlockSpec(wa.shape, fixed),
                  pl.BlockSpec(wb.shape, fixed),
                  pl.BlockSpec((1, D), fixed),
                  pl.BlockSpec((D, n_exp), fixed),
                  pl.BlockSpec((1, n_exp), fixed)],
        out_specs=[pl.BlockSpec((tm, D), row),
                   pl.BlockSpec((tm, D), row),
                   pl.BlockSpec((tm, TOP_K), row),
                   pl.BlockSpec((tm, TOP_K), row),
                   pl.BlockSpec((tm, TOP_K), row),
                   pl.BlockSpec((1, n_exp), fixed)],
        out_shape=[jax.ShapeDtypeStruct((T, D), F32),
                   jax.ShapeDtypeStruct((T, D), F32),
                   jax.ShapeDtypeStruct((T, TOP_K), jnp.int32),
                   jax.ShapeDtypeStruct((T, TOP_K), F32),
                   jax.ShapeDtypeStruct((T, TOP_K), jnp.int32),
                   jax.ShapeDtypeStruct((1, n_exp), F32)],
        scratch_shapes=[pltpu.VMEM((1, n_exp), F32)],
        compiler_params=_cparams("arbitrary"),
        name="outproj_router",
    )(o_da, o_rw, x2, wa, wb, ffn_norm_w.reshape(1, D), router_w, router_b.reshape(1, n_exp))


MOE_ROWS = 512
DISPATCH_TOKENS = 256
ISSUE_UNROLL = 8
MXU_TILE = 256
SUBLANES = 8


def _store_row_tiled(ref, x, lead=()):
    n = x.shape[0]
    for c in range(x.shape[1] // V7X_LANES):
        ref[lead + (pl.ds(c, n, stride=SUBLANES), slice(None))] = x[:, c * V7X_LANES:(c + 1) * V7X_LANES]


def _load_row_tiled(ref, n, c, lead=()):
    return ref[lead + (pl.ds(c, n, stride=SUBLANES), slice(None))]


def _tile_rows(r):
    return pl.multiple_of(r * SUBLANES, SUBLANES)


def _dispatch_body(pad_ref, dest_ref, h_ref, xs_hbm, stage_ref, zero_ref, sem):
    i = pl.program_id(0)

    @pl.when(i == 0)
    def _():
        zero_ref[...] = jnp.zeros(zero_ref.shape, zero_ref.dtype)

        def fill(start_row):
            cp = pltpu.make_async_copy(zero_ref, xs_hbm.at[pl.ds(_tile_rows(start_row), zero_ref.shape[0])], sem)
            cp.start()
            cp.wait()

        n_exp = pad_ref.shape[0] - 1
        lax.fori_loop(0, n_exp, lambda e, c: (fill(pad_ref[e]), c)[1], 0)
        lax.fori_loop(pad_ref[n_exp], xs_hbm.shape[0] // zero_ref.shape[0],
                      lambda b, c: (fill(b * MOE_ROWS), c)[1], 0)

    _store_row_tiled(stage_ref, h_ref[...])

    def issue(t, carry):
        src = stage_ref.at[pl.ds(_tile_rows(t), SUBLANES)]
        for kk in range(TOP_K):
            dst = xs_hbm.at[pl.ds(_tile_rows(dest_ref[0, 0, t * TOP_K + kk]), SUBLANES)]
            pltpu.make_async_copy(src, dst, sem).start()
        return carry

    lax.fori_loop(0, DISPATCH_TOKENS, issue, 0, unroll=ISSUE_UNROLL)
    for _ in range(TOP_K):
        pltpu.make_async_copy(stage_ref, xs_hbm.at[pl.ds(0, stage_ref.shape[0])], sem).wait()


def _dispatch(h, dest, pad_start, n_padded):
    T, D = h.shape
    assert D == SUBLANES * V7X_LANES, "one logical row must be exactly one (8, 128) tile"
    n_blocks = T // DISPATCH_TOKENS
    dest3 = dest.reshape(n_blocks, 1, DISPATCH_TOKENS * TOP_K)
    return pl.pallas_call(
        _dispatch_body,
        grid_spec=pltpu.PrefetchScalarGridSpec(
            num_scalar_prefetch=1,
            grid=(n_blocks,),
            in_specs=[pl.BlockSpec((1, 1, DISPATCH_TOKENS * TOP_K), lambda i, pad: (i, 0, 0),
                                   memory_space=pltpu.SMEM),
                      pl.BlockSpec((DISPATCH_TOKENS, D), lambda i, pad: (i, 0))],
            out_specs=pl.BlockSpec(memory_space=pl.ANY),
            scratch_shapes=[pltpu.VMEM((DISPATCH_TOKENS * SUBLANES, V7X_LANES), h.dtype),
                            pltpu.VMEM((MOE_ROWS * SUBLANES, V7X_LANES), h.dtype),
                            pltpu.SemaphoreType.DMA(())]),
        out_shape=jax.ShapeDtypeStruct((n_padded * SUBLANES, V7X_LANES), h.dtype),
        compiler_params=_cparams("arbitrary"),
        name="moe_dispatch",
    )(pad_start, dest3, h)


def _mlp1_regroup_body(w_ref, o_ref):
    half = MXU_TILE // 2
    r = lax.broadcasted_iota(jnp.int32, (MXU_TILE, MXU_TILE), 0)
    c = lax.broadcasted_iota(jnp.int32, (MXU_TILE, MXU_TILE), 1)
    perm = jnp.where(c == (r >> 1) + (r & 1) * half, 1.0, 0.0).astype(BF16)
    for g in range(w_ref.shape[2] // MXU_TILE):
        sl = slice(g * MXU_TILE, (g + 1) * MXU_TILE)
        o_ref[0, :, sl] = jnp.dot(w_ref[0, :, sl].astype(BF16), perm,
                                  preferred_element_type=F32).astype(o_ref.dtype)


def _mlp1_regroup(mlp1_w):
    n_exp, D, cols = mlp1_w.shape
    tc = _pick_tile(cols, 2 * MXU_TILE)
    blk = lambda e, j: (e, 0, j)
    return pl.pallas_call(
        _mlp1_regroup_body,
        grid=(n_exp, cols // tc),
        in_specs=[pl.BlockSpec((1, D, tc), blk)],
        out_specs=pl.BlockSpec((1, D, tc), blk),
        out_shape=jax.ShapeDtypeStruct(mlp1_w.shape, BF16),
        compiler_params=_cparams("parallel", "parallel"),
        name="mlp1_regroup",
    )(mlp1_w)


def _experts_body(be_ref, nu_ref, xs_ref, w1_ref, b1_ref, w2_ref, b2_ref, y_ref):
    del be_ref
    i = pl.program_id(0)
    half = MXU_TILE // 2

    @pl.when(i < nu_ref[0])
    def _():
        n_rows = xs_ref.shape[0] // SUBLANES
        x = jnp.concatenate([_load_row_tiled(xs_ref, n_rows, c).astype(BF16) for c in range(SUBLANES)], axis=1)
        gu = jnp.dot(x, w1_ref[0], preferred_element_type=F32) + b1_ref[0]
        acts = []
        for g in range(gu.shape[1] // MXU_TILE):
            gate = jnp.minimum(gu[:, g * MXU_TILE:g * MXU_TILE + half], SWIGLU_LIMIT)
            up = jnp.clip(gu[:, g * MXU_TILE + half:(g + 1) * MXU_TILE], -SWIGLU_LIMIT, SWIGLU_LIMIT)
            acts.append(((up + 1.0) * gate * jax.nn.sigmoid(SWIGLU_ALPHA * gate)).astype(BF16))
        act = jnp.concatenate(acts, axis=1)
        _store_row_tiled(y_ref, jnp.dot(act, w2_ref[0], preferred_element_type=F32) + b2_ref[0])

    @pl.when(i >= nu_ref[0])
    def _():
        y_ref[...] = jnp.zeros(y_ref.shape, y_ref.dtype)


def _experts(xs, blk_expert, n_used, w1, b1, w2, b2):
    blk_rows = MOE_ROWS * SUBLANES
    n_blocks = xs.shape[0] // blk_rows
    D, gu_cols = w1.shape[1], w1.shape[2]
    xrow = lambda i, be, nu: (jnp.maximum(jnp.minimum(i, nu[0] - 1), 0), 0)
    yrow = lambda i, be, nu: (i, 0)
    wsel = lambda i, be, nu: (be[i], 0, 0)
    return pl.pallas_call(
        _experts_body,
        grid_spec=pltpu.PrefetchScalarGridSpec(
            num_scalar_prefetch=2,
            grid=(n_blocks,),
            in_specs=[pl.BlockSpec((blk_rows, V7X_LANES), xrow),
                      pl.BlockSpec((1, D, gu_cols), wsel),
                      pl.BlockSpec((1, 1, gu_cols), wsel),
                      pl.BlockSpec((1, gu_cols // 2, D), wsel),
                      pl.BlockSpec((1, 1, D), wsel)],
            out_specs=pl.BlockSpec((blk_rows, V7X_LANES), yrow)),
        out_shape=jax.ShapeDtypeStruct(xs.shape, F32),
        compiler_params=_cparams("arbitrary"),
        name="moe_experts",
    )(blk_expert, n_used, xs, w1, b1, w2, b2)


def _combine_body(dest_ref, next_ref, ys_hbm, x1_ref, gate_ref, fw_ref, o_ref, buf_ref, sems):
    i = pl.program_id(0)
    slot = i % 2

    def gather(d_ref, s):
        def issue(t, carry):
            for kk in range(TOP_K):
                src = ys_hbm.at[pl.ds(_tile_rows(d_ref[0, 0, t * TOP_K + kk]), SUBLANES)]
                pltpu.make_async_copy(src, buf_ref.at[s, kk, pl.ds(_tile_rows(t), SUBLANES)], sems.at[s]).start()
            return carry

        lax.fori_loop(0, DISPATCH_TOKENS, issue, 0, unroll=ISSUE_UNROLL)

    @pl.when(i == 0)
    def _():
        gather(dest_ref, 0)

    @pl.when(i + 1 < pl.num_programs(0))
    def _():
        gather(next_ref, 1 - slot)

    for kk in range(TOP_K):
        pltpu.make_async_copy(ys_hbm.at[pl.ds(0, buf_ref.shape[2])], buf_ref.at[slot, kk], sems.at[slot]).wait()

    gates = gate_ref[...]
    pieces = []
    sq = None
    for c in range(SUBLANES):
        piece = x1_ref[:, c * V7X_LANES:(c + 1) * V7X_LANES]
        for kk in range(TOP_K):
            piece = piece + gates[:, kk:kk + 1] * _load_row_tiled(buf_ref, DISPATCH_TOKENS, c, lead=(slot, kk))
        pieces.append(piece)
        sq = piece * piece if sq is None else sq + piece * piece
    inv = lax.rsqrt(jnp.sum(sq, axis=-1, keepdims=True) * (1.0 / (SUBLANES * V7X_LANES)) + NORM_EPS)
    for c, piece in enumerate(pieces):
        lanes = slice(c * V7X_LANES, (c + 1) * V7X_LANES)
        o_ref[:, lanes] = piece * inv * fw_ref[:, lanes]


def _combine(ys, dest, x1, gates, final_w):
    T, D = x1.shape
    n_blocks = T // DISPATCH_TOKENS
    dest3 = dest.reshape(n_blocks, 1, DISPATCH_TOKENS * TOP_K)
    row = lambda i: (i, 0)
    idx_blk = (1, 1, DISPATCH_TOKENS * TOP_K)
    return pl.pallas_call(
        _combine_body,
        grid=(n_blocks,),
        in_specs=[pl.BlockSpec(idx_blk, lambda i: (i, 0, 0), memory_space=pltpu.SMEM),
                  pl.BlockSpec(idx_blk, lambda i: (jnp.minimum(i + 1, n_blocks - 1), 0, 0),
                               memory_space=pltpu.SMEM),
                  pl.BlockSpec(memory_space=pl.ANY),
                  pl.BlockSpec((DISPATCH_TOKENS, D), row),
                  pl.BlockSpec((DISPATCH_TOKENS, TOP_K), row),
                  pl.BlockSpec((1, D), lambda i: (0, 0))],
        out_specs=pl.BlockSpec((DISPATCH_TOKENS, D), row),
        out_shape=jax.ShapeDtypeStruct((T, D), F32),
        scratch_shapes=[pltpu.VMEM((2, TOP_K, DISPATCH_TOKENS * SUBLANES, V7X_LANES), F32),
                        pltpu.SemaphoreType.DMA((2,))],
        compiler_params=_cparams("arbitrary"),
        name="moe_combine",
    )(dest3, dest3, ys, x1, gates, final_w.reshape(1, D))


def _moe(h, x1, idx, gates, rank, counts, mlp1_w, mlp1_b, mlp2_w, mlp2_b, final_w):
    T, D = h.shape
    n_exp = mlp1_w.shape[0]
    assert T % DISPATCH_TOKENS == 0
    n_padded = T * TOP_K + n_exp * MOE_ROWS
    n_blocks = n_padded // MOE_ROWS
    counts = counts.reshape(n_exp).astype(jnp.int32)
    padded = (counts + MOE_ROWS - 1) // MOE_ROWS * MOE_ROWS
    pend = jnp.cumsum(padded)
    pstart = pend - padded
    dest = (pstart[idx] + rank).astype(jnp.int32)
    starts = jnp.arange(n_blocks, dtype=jnp.int32) * MOE_ROWS
    blk_expert = jnp.minimum(jnp.sum((pend[None, :] <= starts[:, None]).astype(jnp.int32), axis=1),
                             n_exp - 1).astype(jnp.int32)
    n_used = (pend[-1:] // MOE_ROWS).astype(jnp.int32)

    w1 = _mlp1_regroup(mlp1_w)
    half = MXU_TILE // 2
    b1 = mlp1_b.reshape(n_exp, -1, half, 2).transpose(0, 1, 3, 2).reshape(n_exp, 1, -1)
    pad_start = jnp.minimum(pstart + counts, n_padded - MOE_ROWS).astype(jnp.int32)
    xs = _dispatch(h, dest, jnp.concatenate([pad_start, n_used]), n_padded)
    ys = _experts(xs, blk_expert, n_used, w1, b1, mlp2_w.astype(BF16), mlp2_b[:, None, :])
    return _combine(ys, dest, x1, gates, final_w)


def kernel(x, positions, attn_norm_w, w_in, shift_mu, lambda_q1, lambda_k1, lambda_q2, lambda_k2, subln_w,
           rw_w0, rw_w2, rw_a0, rw_a2, rw_g2, rw_k_k, rw_k_a, rw_r_k, rw_lnx_w, rw_lnx_b, w_out, ffn_norm_w,
           router_w, router_b, mlp1_w, mlp1_b, mlp2_w, mlp2_b, final_norm_w):
    B, S, D = x.shape
    assert attn_norm_w.shape[0] == 1, "single-layer stack"
    l = LAYER_INDEX
    x2 = x.reshape(B * S, D)
    cos_t, sin_t = _rope_tables(positions)
    q, k, v, p_rw = _input_projection(x2, attn_norm_w[l], w_in[l], cos_t, sin_t)
    o_da = _diff_attention(q, k, v, lambda_q1[l], lambda_k1[l], lambda_q2[l], lambda_k2[l], subln_w[l], B, S)
    o_rw = _rwkv7(p_rw, shift_mu[l], rw_w0[l], rw_w2[l], rw_a0[l], rw_a2[l], rw_g2[l],
                  rw_k_k[l], rw_k_a[l], rw_r_k[l], rw_lnx_w[l], rw_lnx_b[l], B, S)
    x1, h, idx, gates, rank, counts = _outproj_router(o_da, o_rw, x2, w_out[l], ffn_norm_w[l],
                                                      router_w[l], router_b[l])
    out = _moe(h, x1, idx, gates, rank, counts, mlp1_w[l], mlp1_b[l], mlp2_w[l], mlp2_b[l], final_norm_w)
    return out.reshape(B, S, D)
```

```python
import functools
import math

import jax
import jax.numpy as jnp
from jax import lax
from jax.experimental import pallas as pl
from jax.experimental.pallas import tpu as pltpu

F32 = jnp.float32
BF16 = jnp.bfloat16

V7X_LANES = 128
V7X_VMEM_BYTES = 64 * 1024 * 1024
VMEM_LIMIT_BYTES = 56 * 1024 * 1024

DA_HEADS = 4
DA_HEAD_DIM = 64
DA_V_DIM = 2 * DA_HEAD_DIM
DA_WIDTH = DA_HEADS * DA_V_DIM
RW_HEAD = 64
DECAY_LORA = 64
AAA_LORA = 64
GATE_LORA = 128
ROPE_THETA = 10000.0
N_EXPERTS = 32
TOP_K = 4
SWIGLU_ALPHA = 1.702
SWIGLU_LIMIT = 7.0
NORM_EPS = 1e-5
GN_EPS = 64e-5
LAYER_INDEX = 0
LAM_INIT = 0.8 - 0.6 * math.exp(-0.3 * LAYER_INDEX)


def _cparams(*semantics):
    return pltpu.CompilerParams(dimension_semantics=semantics, vmem_limit_bytes=VMEM_LIMIT_BYTES)


def _pick_tile(n, want):
    t = min(n, want)
    while n % t:
        t //= 2
    return t


def _inproj_body(x_ref, nw_ref, w_ref, cos_ref, sin_ref, q_ref, k_ref, v_ref, prw_ref):
    x = x_ref[...]
    ms = jnp.mean(x * x, axis=-1, keepdims=True)
    h = (x * lax.rsqrt(ms + NORM_EPS)) * nw_ref[...]
    p = jnp.dot(h.astype(BF16), w_ref[...], preferred_element_type=F32)
    cos = cos_ref[...]
    sin = sin_ref[...]
    lane = lax.broadcasted_iota(jnp.int32, cos.shape, 1)
    first_half = (lane % DA_HEAD_DIM) < (DA_HEAD_DIM // 2)
    scale = DA_HEAD_DIM ** -0.5

    def rope(t):
        partner = jnp.where(first_half,
                            pltpu.roll(t, V7X_LANES - DA_HEAD_DIM // 2, 1),
                            pltpu.roll(t, DA_HEAD_DIM // 2, 1))
        return t * cos + partner * sin

    for hh in range(DA_HEADS):
        sl = slice(hh * DA_V_DIM, (hh + 1) * DA_V_DIM)
        q_ref[:, sl] = (rope(p[:, sl]) * scale).astype(q_ref.dtype)
        ksl = slice(DA_WIDTH + hh * DA_V_DIM, DA_WIDTH + (hh + 1) * DA_V_DIM)
        k_ref[:, sl] = rope(p[:, ksl]).astype(k_ref.dtype)
    v_ref[...] = p[:, 2 * DA_WIDTH:3 * DA_WIDTH].astype(v_ref.dtype)
    prw_ref[...] = p[:, 3 * DA_WIDTH:]


def _input_projection(x2, norm_w, w_in, cos_t, sin_t):
    T, D = x2.shape
    n_cols = w_in.shape[1]
    rw_cols = n_cols - 3 * DA_WIDTH
    tm = _pick_tile(T, 512)
    row = lambda i: (i, 0)
    fixed = lambda i: (0, 0)
    return pl.pallas_call(
        _inproj_body,
        grid=(T // tm,),
        in_specs=[pl.BlockSpec((tm, D), row),
                  pl.BlockSpec((1, D), fixed),
                  pl.BlockSpec((D, n_cols), fixed),
                  pl.BlockSpec((tm, DA_V_DIM), row),
                  pl.BlockSpec((tm, DA_V_DIM), row)],
        out_specs=[pl.BlockSpec((tm, DA_WIDTH), row),
                   pl.BlockSpec((tm, DA_WIDTH), row),
                   pl.BlockSpec((tm, DA_WIDTH), row),
                   pl.BlockSpec((tm, rw_cols), row)],
        out_shape=[jax.ShapeDtypeStruct((T, DA_WIDTH), BF16),
                   jax.ShapeDtypeStruct((T, DA_WIDTH), BF16),
                   jax.ShapeDtypeStruct((T, DA_WIDTH), BF16),
                   jax.ShapeDtypeStruct((T, rw_cols), F32)],
        compiler_params=_cparams("parallel"),
        name="input_projection",
    )(x2, norm_w.reshape(1, D), w_in.astype(BF16), cos_t, sin_t)


def _rope_tables(positions):
    half = DA_HEAD_DIM // 2
    inv = ROPE_THETA ** (-jnp.arange(0, DA_HEAD_DIM, 2, dtype=F32) / DA_HEAD_DIM)
    ang = positions.reshape(-1).astype(F32)[:, None] * inv
    cos, sin = jnp.cos(ang), jnp.sin(ang)
    cos_t = jnp.tile(cos, (1, DA_V_DIM // half))
    sin_t = jnp.tile(jnp.concatenate([-sin, sin], axis=-1), (1, DA_V_DIM // DA_HEAD_DIM))
    return cos_t, sin_t


ATTN_Q_BLOCK = 1024
ATTN_K_BLOCK = 1024
ATTN_Q_SUB = 128


def _diff_attn_body(q_ref, k_ref, v_ref, lq1_ref, lk1_ref, lq2_ref, lk2_ref, sw_ref, o_ref,
                    m_ref, acc_ref, *, tq, tk, q_sub):
    qi = pl.program_id(2)
    q = q_ref[0]
    lane = lax.broadcasted_iota(jnp.int32, q.shape, 1)
    zero = jnp.zeros_like(q)
    q_heads = (jnp.where(lane < DA_HEAD_DIM, q, zero), jnp.where(lane >= DA_HEAD_DIM, q, zero))

    m_ref[...] = jnp.full(m_ref.shape, -jnp.inf, F32)
    acc_ref[...] = jnp.zeros(acc_ref.shape, F32)

    def step(j, diag):
        ks = pl.multiple_of(j * tk, tk)
        kb = k_ref[0, pl.ds(ks, tk), :]
        vb = v_ref[0, pl.ds(ks, tk), :]
        v_ext = jnp.concatenate([vb, jnp.ones_like(vb)], axis=1)
        k0 = 0 if diag is None else diag * tk
        units = []
        for r0 in range(0, tq, q_sub):
            n_keys, masked = tk, False
            if diag is not None:
                n_keys = min(tk, r0 + q_sub - k0)
                if n_keys <= 0:
                    continue
                masked = k0 + n_keys - 1 > r0
            units += [(c, r0, n_keys, masked) for c in range(2)]
        scores = [lax.dot_general(q_heads[c][r0:r0 + q_sub], kb[:n_keys], (((1,), (1,)), ((), ())),
                                  preferred_element_type=F32) for c, r0, n_keys, _ in units]
        probs, alphas = [], []
        for (c, r0, n_keys, masked), s in zip(units, scores):
            if masked:
                ri = lax.broadcasted_iota(jnp.int32, s.shape, 0) + r0
                ci = lax.broadcasted_iota(jnp.int32, s.shape, 1) + k0
                s = jnp.where(ci <= ri, s, -jnp.inf)
            m_prev = m_ref[c, r0:r0 + q_sub, :]
            m_new = jnp.maximum(m_prev, jnp.max(s, axis=-1, keepdims=True))
            m_ref[c, r0:r0 + q_sub, :] = m_new
            alphas.append(jnp.exp(m_prev - m_new))
            probs.append(jnp.concatenate(
                [jnp.exp(s[:, t * V7X_LANES:(t + 1) * V7X_LANES] - m_new).astype(vb.dtype)
                 for t in range(n_keys // V7X_LANES)], axis=1))
        for (c, r0, n_keys, _), p, alpha in zip(units, probs, alphas):
            pv = jnp.dot(p, v_ext[:n_keys], preferred_element_type=F32)
            rows = slice(r0, r0 + q_sub)
            acc_ref[c, rows, :] = jnp.concatenate([alpha, alpha], axis=1) * acc_ref[c, rows, :] + pv

    def full_step(j, carry):
        step(j, None)
        return carry

    lax.fori_loop(0, qi * (tq // tk), full_step, 0)
    for d in range(tq // tk):
        step(qi * (tq // tk) + d, d)

    lam = (jnp.exp(jnp.sum(lq1_ref[...] * lk1_ref[...], axis=-1, keepdims=True))
           - jnp.exp(jnp.sum(lq2_ref[...] * lk2_ref[...], axis=-1, keepdims=True)) + LAM_INIT)
    a0, a1 = acc_ref[0], acc_ref[1]
    o = a0[:, :DA_V_DIM] / a0[:, DA_V_DIM:] - lam * (a1[:, :DA_V_DIM] / a1[:, DA_V_DIM:])
    o = o * lax.rsqrt(jnp.mean(o * o, axis=-1, keepdims=True) + NORM_EPS)
    o_ref[0] = (o * sw_ref[...] * (1.0 - LAM_INIT)).astype(o_ref.dtype)


def _diff_attention(q, k, v, lq1, lk1, lq2, lk2, subln_w, B, S):
    tq = _pick_tile(S, ATTN_Q_BLOCK)
    tk = _pick_tile(tq, ATTN_K_BLOCK)
    q_sub = _pick_tile(tk, ATTN_Q_SUB)
    q3, k3, v3 = (t.reshape(B, S, DA_WIDTH) for t in (q, k, v))
    vec = lambda a: a.reshape(1, -1).astype(F32)
    blk = lambda b, h, i: (b, i, h)
    seq = lambda b, h, i: (b, 0, h)
    fixed = lambda b, h, i: (0, 0)
    out = pl.pallas_call(
        functools.partial(_diff_attn_body, tq=tq, tk=tk, q_sub=q_sub),
        grid=(B, DA_HEADS, S // tq),
        in_specs=[pl.BlockSpec((1, tq, DA_V_DIM), blk),
                  pl.BlockSpec((1, S, DA_V_DIM), seq),
                  pl.BlockSpec((1, S, DA_V_DIM), seq),
                  pl.BlockSpec((1, DA_HEAD_DIM), fixed),
                  pl.BlockSpec((1, DA_HEAD_DIM), fixed),
                  pl.BlockSpec((1, DA_HEAD_DIM), fixed),
                  pl.BlockSpec((1, DA_HEAD_DIM), fixed),
                  pl.BlockSpec((1, DA_V_DIM), fixed)],
        out_specs=pl.BlockSpec((1, tq, DA_V_DIM), blk),
        out_shape=jax.ShapeDtypeStruct((B, S, DA_WIDTH), BF16),
        scratch_shapes=[pltpu.VMEM((2, tq, V7X_LANES), F32),
                        pltpu.VMEM((2, tq, 2 * DA_V_DIM), F32)],
        compiler_params=_cparams("parallel", "parallel", "arbitrary"),
        name="diff_attention",
    )(q3, k3, v3, vec(lq1), vec(lk1), vec(lq2), vec(lk2), vec(subln_w))
    return out.reshape(B * S, DA_WIDTH)


RW_CHUNK = 64
RW_PAIR = 2 * RW_HEAD
RW_INV_DOUBLINGS = 5


def _bdot(a, b):
    return jnp.dot(a.astype(BF16), b.astype(BF16), preferred_element_type=F32)


def _bdot_nt(a, b):
    return lax.dot_general(a.astype(BF16), b.astype(BF16), (((1,), (1,)), ((), ())),
                           preferred_element_type=F32)


def _bdot_tn(a, b):
    return lax.dot_general(a.astype(BF16), b.astype(BF16), (((0,), (0,)), ((), ())),
                           preferred_element_type=F32)


def _split3_dot(m, x):
    mb = m.astype(BF16)
    x0 = x.astype(BF16)
    r1 = x - x0.astype(F32)
    x1 = r1.astype(BF16)
    x2 = (r1 - x1.astype(F32)).astype(BF16)
    dot = lambda t: jnp.dot(mb, t, preferred_element_type=F32)
    return dot(x0) + dot(x1) + dot(x2)


def _rwkv_body(p_ref, prev_ref, mu_ref, w0_ref, a0_ref, kk_ref, ka_ref, rk_ref, lnw_ref, lnb_ref,
               wa_ref, g2_ref, o_ref,
               s_ref, rt_ref, kh_ref, at_ref, bh_ref, v_ref, gl_ref, y_ref, *, tb, width):
    i = pl.program_id(1)
    n_pairs = width // RW_PAIR
    C = RW_CHUNK

    @pl.when(i == 0)
    def _():
        s_ref[...] = jnp.zeros(s_ref.shape, F32)

    p = p_ref[0]
    row = lax.broadcasted_iota(jnp.int32, p.shape, 0)
    last_prev = jnp.where(i == 0, 0.0, prev_ref[0, 7:8, :])
    prev = jnp.where(row == 0, last_prev, pltpu.roll(p, 1, 0))
    ps = p + mu_ref[...] * (prev - p)
    r = ps[:, 0:width]
    k = ps[:, width:2 * width]
    v = ps[:, 2 * width:3 * width]
    c3 = 3 * width
    wa_in = ps[:, c3:c3 + DECAY_LORA + AAA_LORA]
    lane = lax.broadcasted_iota(jnp.int32, wa_in.shape, 1)
    wa_in = jnp.where(lane < DECAY_LORA, jnp.tanh(wa_in), wa_in)
    wa = _bdot(wa_in, wa_ref[...])
    gd = ps[:, c3 + DECAY_LORA + AAA_LORA:]
    g = _bdot(jax.nn.sigmoid(gd), g2_ref[...])

    wl = w0_ref[...] + wa[:, :width]
    softplus_neg = jnp.maximum(-wl, 0.0) + jnp.log1p(jnp.exp(-jnp.abs(wl)))
    logw = -jnp.exp(-softplus_neg - 0.5)
    a = jax.nn.sigmoid(a0_ref[...] + wa[:, width:])

    hr = lax.broadcasted_iota(jnp.int32, (width, width), 0) // RW_HEAD
    hc = lax.broadcasted_iota(jnp.int32, (width, width), 1) // RW_HEAD
    head_ones = jnp.where(hr == hc, 1.0, 0.0).astype(BF16)
    head_sum = lambda t: jnp.dot(t.astype(BF16), head_ones, preferred_element_type=F32)

    kk = k * kk_ref[...]
    kk = kk / jnp.maximum(jnp.sqrt(head_sum(kk * kk)), 1e-12)
    k2 = k * (1.0 + (a - 1.0) * ka_ref[...])
    bonus = head_sum(r * k2 * rk_ref[...]) * v

    tr = lax.broadcasted_iota(jnp.int32, (tb, tb), 0)
    tc = lax.broadcasted_iota(jnp.int32, (tb, tb), 1)
    chunk_tril = jnp.where((tr // C == tc // C) & (tc <= tr), 1.0, 0.0)
    cum = _split3_dot(chunk_tril, logw)
    g_inc = jnp.exp(cum)
    g_inv = jnp.exp(-cum)
    g_exc = jnp.exp(cum - logw)
    rt_ref[...] = r * g_inc
    kh_ref[...] = k2 * g_inv
    at_ref[...] = -kk * g_exc
    bh_ref[...] = kk * a * g_inv
    v_ref[...] = v
    gl_ref[...] = g_inc

    lane_p = lax.broadcasted_iota(jnp.int32, (C, RW_PAIR), 1)
    first = lane_p < RW_HEAD
    ri = lax.broadcasted_iota(jnp.int32, (RW_PAIR, RW_PAIR), 0)
    ci = lax.broadcasted_iota(jnp.int32, (RW_PAIR, RW_PAIR), 1)
    strict = ci < ri
    incl = ci <= ri
    eye = jnp.where(ci == ri, 1.0, 0.0)

    def stack(t):
        return jnp.concatenate([jnp.where(first, t, 0.0), jnp.where(first, 0.0, t)], axis=0)

    n_chunks = tb // C
    units = [(c, j) for c in range(n_chunks) for j in range(n_pairs)]
    U = range(len(units))
    rows = [slice(c * C, (c + 1) * C) for c, _ in units]
    lanes = [slice(j * RW_PAIR, (j + 1) * RW_PAIR) for _, j in units]
    rts = [stack(rt_ref[rows[n], lanes[n]]) for n in U]
    khs = [stack(kh_ref[rows[n], lanes[n]]) for n in U]
    ats = [stack(at_ref[rows[n], lanes[n]]) for n in U]
    bhs = [stack(bh_ref[rows[n], lanes[n]]) for n in U]
    vs = [stack(v_ref[rows[n], lanes[n]]) for n in U]
    g_last = [gl_ref[(c + 1) * C - 1:(c + 1) * C, lanes[n]] for n, (c, _) in enumerate(units)]

    sc = [_bdot_nt(jnp.concatenate([ats[n], rts[n]], axis=0), jnp.concatenate([bhs[n], khs[n]], axis=0))
          for n in U]
    l_ab = [jnp.where(strict, sc[n][:RW_PAIR, :RW_PAIR], 0.0) for n in U]
    l_ak = [jnp.where(strict, sc[n][:RW_PAIR, RW_PAIR:], 0.0) for n in U]
    a_rb = [jnp.where(incl, sc[n][RW_PAIR:, :RW_PAIR], 0.0) for n in U]
    a_rk = [jnp.where(incl, sc[n][RW_PAIR:, RW_PAIR:], 0.0) for n in U]

    pw = [_bdot(l_ab[n], l_ab[n]) for n in U]
    tinv = [eye + l_ab[n] for n in U]
    for _ in range(1, RW_INV_DOUBLINGS):
        both = [_bdot(jnp.concatenate([pw[n], tinv[n]], axis=0), pw[n]) for n in U]
        tinv = [tinv[n] + both[n][RW_PAIR:] for n in U]
        pw = [both[n][:RW_PAIR] for n in U]
    tinv = [tinv[n] + _bdot(tinv[n], pw[n]) for n in U]

    lv = [_bdot(l_ak[n], vs[n]) for n in U]
    g0 = [_bdot_tn(vs[n], khs[n]) for n in U]
    taw = [_bdot(tinv[n], jnp.concatenate([ats[n], lv[n]], axis=1)) for n in U]
    ar = [_bdot(jnp.concatenate([a_rb[n], a_rk[n]], axis=1),
                jnp.concatenate([taw[n], jnp.concatenate([jnp.zeros_like(vs[n]), vs[n]], axis=1)], axis=0))
          for n in U]
    qt_ta = [jnp.concatenate([rts[n] + ar[n][:, :RW_PAIR], taw[n][:, :RW_PAIR]], axis=0) for n in U]

    state = [s_ref[j] for j in range(n_pairs)]
    for c in range(n_chunks):
        ns = [c * n_pairs + j for j in range(n_pairs)]
        m1 = [_bdot_nt(qt_ta[n], state[j]) for j, n in enumerate(ns)]
        u = [m1[j][RW_PAIR:] + taw[n][:, RW_PAIR:] for j, n in enumerate(ns)]
        state = [(state[j] + g0[n] + _bdot_tn(u[j], bhs[n])) * g_last[n] for j, n in enumerate(ns)]
        for j, n in enumerate(ns):
            y = m1[j][:RW_PAIR] + ar[n][:, RW_PAIR:]
            y_ref[rows[n], lanes[n]] = y[:C] + y[C:]
    for j in range(n_pairs):
        s_ref[j] = state[j]

    y = y_ref[...]
    inv_n = 1.0 / RW_HEAD
    mean = head_sum(y) * inv_n
    yc = y - mean
    var = head_sum(yc * yc) * inv_n
    yn = yc * lax.rsqrt(var + GN_EPS) * lnw_ref[...] + lnb_ref[...]
    o_ref[0] = ((yn + bonus) * g).astype(o_ref.dtype)


def _rwkv7(p_rw, shift_mu, w0, w2, a0, a2, g2, k_k, k_a, r_k, lnx_w, lnx_b, B, S):
    cols = p_rw.shape[-1]
    width = (cols - DECAY_LORA - AAA_LORA - GATE_LORA) // 3
    tb = _pick_tile(S, 256)
    assert tb % RW_CHUNK == 0 and width % RW_PAIR == 0
    p3 = p_rw.reshape(B, S, cols)
    vec = lambda t: t.reshape(1, -1).astype(F32)
    zeros = jnp.zeros_like(w2)
    wa = jnp.concatenate([jnp.concatenate([w2, zeros], axis=1),
                          jnp.concatenate([jnp.zeros_like(a2), a2], axis=1)], axis=0).astype(BF16)
    blk = lambda b, i: (b, i, 0)
    prev_blk = lambda b, i: (b, jnp.maximum(i * (tb // 8) - 1, 0), 0)
    fixed = lambda b, i: (0, 0)
    wide = pltpu.VMEM((tb, width), F32)
    out = pl.pallas_call(
        functools.partial(_rwkv_body, tb=tb, width=width),
        grid=(B, S // tb),
        in_specs=[pl.BlockSpec((1, tb, cols), blk),
                  pl.BlockSpec((1, 8, cols), prev_blk),
                  pl.BlockSpec((1, cols), fixed)]
                 + [pl.BlockSpec((1, width), fixed)] * 7
                 + [pl.BlockSpec((DECAY_LORA + AAA_LORA, 2 * width), fixed),
                    pl.BlockSpec((GATE_LORA, width), fixed)],
        out_specs=pl.BlockSpec((1, tb, width), blk),
        out_shape=jax.ShapeDtypeStruct((B, S, width), BF16),
        scratch_shapes=[pltpu.VMEM((width // RW_PAIR, RW_PAIR, RW_PAIR), F32)] + [wide] * 7,
        compiler_params=_cparams("parallel", "arbitrary"),
        name="rwkv7_mix",
    )(p3, p3, vec(shift_mu), vec(w0), vec(a0), vec(k_k), vec(k_a), vec(r_k), vec(lnx_w), vec(lnx_b),
      wa, g2.astype(BF16))
    return out.reshape(B * S, width)


def _outproj_router_body(oda_ref, orw_ref, x_ref, wa_ref, wb_ref, nw_ref, rw_ref, rb_ref,
                         x1_ref, h_ref, idx_ref, gate_ref, rank_ref, cnt_ref, carry_ref):
    i = pl.program_id(0)

    @pl.when(i == 0)
    def _():
        carry_ref[...] = jnp.zeros(carry_ref.shape, F32)

    x1 = (x_ref[...] + jnp.dot(oda_ref[...], wa_ref[...], preferred_element_type=F32)
          + jnp.dot(orw_ref[...], wb_ref[...], preferred_element_type=F32))
    x1_ref[...] = x1
    h = (x1 * lax.rsqrt(jnp.mean(x1 * x1, axis=-1, keepdims=True) + NORM_EPS)) * nw_ref[...]
    h_ref[...] = h
    h_hi = h.astype(BF16)
    h_lo = (h - h_hi.astype(F32)).astype(BF16)
    rw = rw_ref[...]
    rw_hi = rw.astype(BF16)
    rw_lo = (rw - rw_hi.astype(F32)).astype(BF16)
    logits = (jnp.dot(h_hi, rw_hi, preferred_element_type=F32) + jnp.dot(h_lo, rw_hi, preferred_element_type=F32)
              + jnp.dot(h_hi, rw_lo, preferred_element_type=F32) + rb_ref[...])
    tm, n_exp = logits.shape
    lane = lax.broadcasted_iota(jnp.int32, logits.shape, 1).astype(F32)
    out_lane = lax.broadcasted_iota(jnp.int32, (tm, TOP_K), 1)

    vals, idxs = [], []
    rest = logits
    for _ in range(TOP_K):
        m = jnp.max(rest, axis=-1, keepdims=True)
        sel = jnp.min(jnp.where(rest == m, lane, float(n_exp)), axis=-1, keepdims=True)
        vals.append(m)
        idxs.append(sel)
        rest = jnp.where(lane == sel, -jnp.inf, rest)
    exps = [jnp.exp(v - vals[0]) for v in vals]
    denom = exps[0]
    for e in exps[1:]:
        denom = denom + e

    hot = [jnp.where(lane == s, 1.0, 0.0) for s in idxs]
    hot_all = hot[0]
    for t in hot[1:]:
        hot_all = hot_all + t
    tr = lax.broadcasted_iota(jnp.int32, (tm, tm), 0)
    tc = lax.broadcasted_iota(jnp.int32, (tm, tm), 1)
    before = jnp.dot(jnp.where(tc < tr, 1.0, 0.0).astype(BF16), hot_all.astype(BF16),
                     preferred_element_type=F32) + carry_ref[...]

    idx_out = jnp.zeros((tm, TOP_K), F32)
    gate_out = jnp.zeros((tm, TOP_K), F32)
    rank_out = jnp.zeros((tm, TOP_K), F32)
    for kk in range(TOP_K):
        rank_k = jnp.sum(hot[kk] * before, axis=-1, keepdims=True)
        idx_out = jnp.where(out_lane == kk, idxs[kk], idx_out)
        gate_out = jnp.where(out_lane == kk, exps[kk] / denom, gate_out)
        rank_out = jnp.where(out_lane == kk, rank_k, rank_out)
    idx_ref[...] = idx_out.astype(jnp.int32)
    gate_ref[...] = gate_out
    rank_ref[...] = rank_out.astype(jnp.int32)
    carry_ref[...] += jnp.sum(hot_all, axis=0, keepdims=True)
    cnt_ref[...] = carry_ref[...]


def _outproj_router(o_da, o_rw, x2, w_out, ffn_norm_w, router_w, router_b):
    T, D = x2.shape
    n_exp = router_w.shape[1]
    tm = _pick_tile(T, 512)
    wa = w_out[:DA_WIDTH].astype(BF16)
    wb = w_out[DA_WIDTH:].astype(BF16)
    row = lambda i: (i, 0)
    fixed = lambda i: (0, 0)
    return pl.pallas_call(
        _outproj_router_body,
        grid=(T // tm,),
        in_specs=[pl.BlockSpec((tm, DA_WIDTH), row),
                  pl.BlockSpec((tm, o_rw.shape[1]), row),
                  pl.BlockSpec((tm, D), row),
                  pl.BlockSpec(wa.shape, fixed),
                  pl.BlockSpec(wb.shape, fixed),
                  pl.BlockSpec((1, D), fixed),
                  pl.BlockSpec((D, n_exp), fixed),
                  pl.BlockSpec((1, n_exp), fixed)],
        out_specs=[pl.BlockSpec((tm, D), row),
                   pl.BlockSpec((tm, D), row),
                   pl.BlockSpec((tm, TOP_K), row),
                   pl.BlockSpec((tm, TOP_K), row),
                   pl.BlockSpec((tm, TOP_K), row),
                   pl.BlockSpec((1, n_exp), fixed)],
        out_shape=[jax.ShapeDtypeStruct((T, D), F32),
                   jax.ShapeDtypeStruct((T, D), F32),
                   jax.ShapeDtypeStruct((T, TOP_K), jnp.int32),
                   jax.ShapeDtypeStruct((T, TOP_K), F32),
                   jax.ShapeDtypeStruct((T, TOP_K), jnp.int32),
                   jax.ShapeDtypeStruct((1, n_exp), F32)],
        scratch_shapes=[pltpu.VMEM((1, n_exp), F32)],
        compiler_params=_cparams("arbitrary"),
        name="outproj_router",
    )(o_da, o_rw, x2, wa, wb, ffn_norm_w.reshape(1, D), router_w, router_b.reshape(1, n_exp))


MOE_ROWS = 512
DISPATCH_TOKENS = 256
ISSUE_UNROLL = 8
MXU_TILE = 256
SUBLANES = 8


def _store_row_tiled(ref, x, lead=()):
    n = x.shape[0]
    for c in range(x.shape[1] // V7X_LANES):
        ref[lead + (pl.ds(c, n, stride=SUBLANES), slice(None))] = x[:, c * V7X_LANES:(c + 1) * V7X_LANES]


def _load_row_tiled(ref, n, c, lead=()):
    return ref[lead + (pl.ds(c, n, stride=SUBLANES), slice(None))]


def _tile_rows(r):
    return pl.multiple_of(r * SUBLANES, SUBLANES)


def _dispatch_body(pad_ref, dest_ref, h_ref, xs_hbm, stage_ref, zero_ref, sems, zero_sem):
    i = pl.program_id(0)

    @pl.when(i == 0)
    def _():
        zero_ref[...] = jnp.zeros(zero_ref.shape, zero_ref.dtype)

        def fill(start_row):
            cp = pltpu.make_async_copy(zero_ref, xs_hbm.at[pl.ds(_tile_rows(start_row), zero_ref.shape[0])],
                                       zero_sem)
            cp.start()
            cp.wait()

        def fill_pad(e, carry):
            fill(pad_ref[e])
            return carry

        def fill_tail(b, carry):
            fill(b * MOE_ROWS)
            return carry

        n_exp = pad_ref.shape[0] - 1
        lax.fori_loop(0, n_exp, fill_pad, 0)
        lax.fori_loop(pad_ref[n_exp], xs_hbm.shape[0] // zero_ref.shape[0], fill_tail, 0)

    slot = i % 2
    _store_row_tiled(stage_ref, h_ref[...], lead=(slot,))

    def issue(t, carry):
        src = stage_ref.at[slot, pl.ds(_tile_rows(t), SUBLANES)]
        for kk in range(TOP_K):
            dst = xs_hbm.at[pl.ds(_tile_rows(dest_ref[0, 0, t * TOP_K + kk]), SUBLANES)]
            pltpu.make_async_copy(src, dst, sems.at[slot]).start()
        return carry

    lax.fori_loop(0, DISPATCH_TOKENS, issue, 0, unroll=ISSUE_UNROLL)

    def drain(s):
        for _ in range(TOP_K):
            pltpu.make_async_copy(stage_ref.at[s], xs_hbm.at[pl.ds(0, stage_ref.shape[1])], sems.at[s]).wait()

    @pl.when(i > 0)
    def _():
        drain(1 - slot)

    @pl.when(i == pl.num_programs(0) - 1)
    def _():
        drain(slot)


def _dispatch(h, dest, pad_start, n_padded):
    T, D = h.shape
    assert D == SUBLANES * V7X_LANES, "one logical row must be exactly one (8, 128) tile"
    n_blocks = T // DISPATCH_TOKENS
    dest3 = dest.reshape(n_blocks, 1, DISPATCH_TOKENS * TOP_K)
    return pl.pallas_call(
        _dispatch_body,
        grid_spec=pltpu.PrefetchScalarGridSpec(
            num_scalar_prefetch=1,
            grid=(n_blocks,),
            in_specs=[pl.BlockSpec((1, 1, DISPATCH_TOKENS * TOP_K), lambda i, pad: (i, 0, 0),
                                   memory_space=pltpu.SMEM),
                      pl.BlockSpec((DISPATCH_TOKENS, D), lambda i, pad: (i, 0))],
            out_specs=pl.BlockSpec(memory_space=pl.ANY),
            scratch_shapes=[pltpu.VMEM((2, DISPATCH_TOKENS * SUBLANES, V7X_LANES), h.dtype),
                            pltpu.VMEM((MOE_ROWS * SUBLANES, V7X_LANES), h.dtype),
                            pltpu.SemaphoreType.DMA((2,)),
                            pltpu.SemaphoreType.DMA(())]),
        out_shape=jax.ShapeDtypeStruct((n_padded * SUBLANES, V7X_LANES), h.dtype),
        compiler_params=_cparams("arbitrary"),
        name="moe_dispatch",
    )(pad_start, dest3, h)


def _expert_weights_body(w1_ref, w2_ref, o1_ref, o2_ref):
    half = MXU_TILE // 2
    r = lax.broadcasted_iota(jnp.int32, (MXU_TILE, MXU_TILE), 0)
    c = lax.broadcasted_iota(jnp.int32, (MXU_TILE, MXU_TILE), 1)
    perm = jnp.where(c == (r >> 1) + (r & 1) * half, 1.0, 0.0).astype(BF16)
    for g in range(w1_ref.shape[2] // MXU_TILE):
        sl = slice(g * MXU_TILE, (g + 1) * MXU_TILE)
        o1_ref[0, :, sl] = jnp.dot(w1_ref[0, :, sl].astype(BF16), perm,
                                   preferred_element_type=F32).astype(o1_ref.dtype)
    o2_ref[...] = w2_ref[...].astype(o2_ref.dtype)


def _expert_weights(mlp1_w, mlp2_w):
    n_exp = mlp1_w.shape[0]
    blk = lambda e: (e, 0, 0)
    spec1 = pl.BlockSpec((1,) + mlp1_w.shape[1:], blk)
    spec2 = pl.BlockSpec((1,) + mlp2_w.shape[1:], blk)
    return pl.pallas_call(
        _expert_weights_body,
        grid=(n_exp,),
        in_specs=[spec1, spec2],
        out_specs=[spec1, spec2],
        out_shape=[jax.ShapeDtypeStruct(mlp1_w.shape, BF16), jax.ShapeDtypeStruct(mlp2_w.shape, BF16)],
        compiler_params=_cparams("parallel"),
        name="expert_weights",
    )(mlp1_w, mlp2_w)


def _experts_body(be_ref, nu_ref, xs_ref, w1_ref, b1_ref, w2_ref, b2_ref, y_ref):
    del be_ref
    i = pl.program_id(0)
    half = MXU_TILE // 2

    @pl.when(i < nu_ref[0])
    def _():
        n_rows = xs_ref.shape[0] // SUBLANES
        x = jnp.concatenate([_load_row_tiled(xs_ref, n_rows, c).astype(BF16) for c in range(SUBLANES)], axis=1)
        gu = jnp.dot(x, w1_ref[0], preferred_element_type=F32) + b1_ref[0]
        acts = []
        for g in range(gu.shape[1] // MXU_TILE):
            gate = jnp.minimum(gu[:, g * MXU_TILE:g * MXU_TILE + half], SWIGLU_LIMIT)
            up = jnp.clip(gu[:, g * MXU_TILE + half:(g + 1) * MXU_TILE], -SWIGLU_LIMIT, SWIGLU_LIMIT)
            acts.append(((up + 1.0) * gate * jax.nn.sigmoid(SWIGLU_ALPHA * gate)).astype(BF16))
        act = jnp.concatenate(acts, axis=1)
        _store_row_tiled(y_ref, jnp.dot(act, w2_ref[0], preferred_element_type=F32) + b2_ref[0])

    @pl.when(i >= nu_ref[0])
    def _():
        y_ref[...] = jnp.zeros(y_ref.shape, y_ref.dtype)


def _experts(xs, blk_expert, n_used, w1, b1, w2, b2):
    blk_rows = MOE_ROWS * SUBLANES
    n_blocks = xs.shape[0] // blk_rows
    D, gu_cols = w1.shape[1], w1.shape[2]
    xrow = lambda i, be, nu: (jnp.maximum(jnp.minimum(i, nu[0] - 1), 0), 0)
    yrow = lambda i, be, nu: (i, 0)
    wsel = lambda i, be, nu: (be[i], 0, 0)
    return pl.pallas_call(
        _experts_body,
        grid_spec=pltpu.PrefetchScalarGridSpec(
            num_scalar_prefetch=2,
            grid=(n_blocks,),
            in_specs=[pl.BlockSpec((blk_rows, V7X_LANES), xrow),
                      pl.BlockSpec((1, D, gu_cols), wsel),
                      pl.BlockSpec((1, 1, gu_cols), wsel),
                      pl.BlockSpec((1, gu_cols // 2, D), wsel),
                      pl.BlockSpec((1, 1, D), wsel)],
            out_specs=pl.BlockSpec((blk_rows, V7X_LANES), yrow)),
        out_shape=jax.ShapeDtypeStruct(xs.shape, F32),
        compiler_params=_cparams("arbitrary"),
        name="moe_experts",
    )(blk_expert, n_used, xs, w1, b1, w2, b2)


def _combine_body(dest_ref, next_ref, ys_hbm, x1_ref, gate_ref, fw_ref, o_ref, buf_ref, sems):
    i = pl.program_id(0)
    slot = i % 2

    def gather(d_ref, s):
        def issue(t, carry):
            for kk in range(TOP_K):
                src = ys_hbm.at[pl.ds(_tile_rows(d_ref[0, 0, t * TOP_K + kk]), SUBLANES)]
                pltpu.make_async_copy(src, buf_ref.at[s, kk, pl.ds(_tile_rows(t), SUBLANES)], sems.at[s]).start()
            return carry

        lax.fori_loop(0, DISPATCH_TOKENS, issue, 0, unroll=ISSUE_UNROLL)

    @pl.when(i == 0)
    def _():
        gather(dest_ref, 0)

    @pl.when(i + 1 < pl.num_programs(0))
    def _():
        gather(next_ref, 1 - slot)

    for kk in range(TOP_K):
        pltpu.make_async_copy(ys_hbm.at[pl.ds(0, buf_ref.shape[2])], buf_ref.at[slot, kk], sems.at[slot]).wait()

    gates = gate_ref[...]
    pieces = []
    sq = None
    for c in range(SUBLANES):
        piece = x1_ref[:, c * V7X_LANES:(c + 1) * V7X_LANES]
        for kk in range(TOP_K):
            piece = piece + gates[:, kk:kk + 1] * _load_row_tiled(buf_ref, DISPATCH_TOKENS, c, lead=(slot, kk))
        pieces.append(piece)
        sq = piece * piece if sq is None else sq + piece * piece
    inv = lax.rsqrt(jnp.sum(sq, axis=-1, keepdims=True) * (1.0 / (SUBLANES * V7X_LANES)) + NORM_EPS)
    for c, piece in enumerate(pieces):
        lanes = slice(c * V7X_LANES, (c + 1) * V7X_LANES)
        o_ref[:, lanes] = piece * inv * fw_ref[:, lanes]


def _combine(ys, dest, x1, gates, final_w):
    T, D = x1.shape
    n_blocks = T // DISPATCH_TOKENS
    dest3 = dest.reshape(n_blocks, 1, DISPATCH_TOKENS * TOP_K)
    row = lambda i: (i, 0)
    idx_blk = (1, 1, DISPATCH_TOKENS * TOP_K)
    return pl.pallas_call(
        _combine_body,
        grid=(n_blocks,),
        in_specs=[pl.BlockSpec(idx_blk, lambda i: (i, 0, 0), memory_space=pltpu.SMEM),
                  pl.BlockSpec(idx_blk, lambda i: (jnp.minimum(i + 1, n_blocks - 1), 0, 0),
                               memory_space=pltpu.SMEM),
                  pl.BlockSpec(memory_space=pl.ANY),
                  pl.BlockSpec((DISPATCH_TOKENS, D), row),
                  pl.BlockSpec((DISPATCH_TOKENS, TOP_K), row),
                  pl.BlockSpec((1, D), lambda i: (0, 0))],
        out_specs=pl.BlockSpec((DISPATCH_TOKENS, D), row),
        out_shape=jax.ShapeDtypeStruct((T, D), F32),
        scratch_shapes=[pltpu.VMEM((2, TOP_K, DISPATCH_TOKENS * SUBLANES, V7X_LANES), F32),
                        pltpu.SemaphoreType.DMA((2,))],
        compiler_params=_cparams("arbitrary"),
        name="moe_combine",
    )(dest3, dest3, ys, x1, gates, final_w.reshape(1, D))


def _moe(h, x1, idx, gates, rank, counts, mlp1_w, mlp1_b, mlp2_w, mlp2_b, final_w):
    T, D = h.shape
    n_exp = mlp1_w.shape[0]
    assert T % DISPATCH_TOKENS == 0
    n_padded = T * TOP_K + n_exp * MOE_ROWS
    n_blocks = n_padded // MOE_ROWS
    counts = counts.reshape(n_exp).astype(jnp.int32)
    padded = (counts + MOE_ROWS - 1) // MOE_ROWS * MOE_ROWS
    pend = jnp.cumsum(padded)
    pstart = pend - padded
    dest = (pstart[idx] + rank).astype(jnp.int32)
    starts = jnp.arange(n_blocks, dtype=jnp.int32) * MOE_ROWS
    blk_expert = jnp.minimum(jnp.sum((pend[None, :] <= starts[:, None]).astype(jnp.int32), axis=1),
                             n_exp - 1).astype(jnp.int32)
    n_used = (pend[-1:] // MOE_ROWS).astype(jnp.int32)

    w1, w2 = _expert_weights(mlp1_w, mlp2_w)
    half = MXU_TILE // 2
    b1 = mlp1_b.reshape(n_exp, -1, half, 2).transpose(0, 1, 3, 2).reshape(n_exp, 1, -1)
    pad_start = jnp.minimum(pstart + counts, n_padded - MOE_ROWS).astype(jnp.int32)
    xs = _dispatch(h, dest, jnp.concatenate([pad_start, n_used]), n_padded)
    ys = _experts(xs, blk_expert, n_used, w1, b1, w2, mlp2_b[:, None, :])
    return _combine(ys, dest, x1, gates, final_w)


def kernel(x, positions, attn_norm_w, w_in, shift_mu, lambda_q1, lambda_k1, lambda_q2, lambda_k2, subln_w,
           rw_w0, rw_w2, rw_a0, rw_a2, rw_g2, rw_k_k, rw_k_a, rw_r_k, rw_lnx_w, rw_lnx_b, w_out, ffn_norm_w,
           router_w, router_b, mlp1_w, mlp1_b, mlp2_w, mlp2_b, final_norm_w):
    B, S, D = x.shape
    assert attn_norm_w.shape[0] == 1, "single-layer stack"
    l = LAYER_INDEX
    x2 = x.reshape(B * S, D)
    cos_t, sin_t = _rope_tables(positions)
    q, k, v, p_rw = _input_projection(x2, attn_norm_w[l], w_in[l], cos_t, sin_t)
    o_da = _diff_attention(q, k, v, lambda_q1[l], lambda_k1[l], lambda_q2[l], lambda_k2[l], subln_w[l], B, S)
    o_rw = _rwkv7(p_rw, shift_mu[l], rw_w0[l], rw_w2[l], rw_a0[l], rw_a2[l], rw_g2[l],
                  rw_k_k[l], rw_k_a[l], rw_r_k[l], rw_lnx_w[l], rw_lnx_b[l], B, S)
    x1, h, idx, gates, rank, counts = _outproj_router(o_da, o_rw, x2, w_out[l], ffn_norm_w[l],
                                                      router_w[l], router_b[l])
    out = _moe(h, x1, idx, gates, rank, counts, mlp1_w[l], mlp1_b[l], mlp2_w[l], mlp2_b[l], final_norm_w)
    return out.reshape(B, S, D)
```

```python
import functools
import math

import jax
import jax.numpy as jnp
from jax import lax
from jax.experimental import pallas as pl
from jax.experimental.pallas import tpu as pltpu

F32 = jnp.float32
BF16 = jnp.bfloat16

V7X_LANES = 128
V7X_VMEM_BYTES = 64 * 1024 * 1024
VMEM_LIMIT_BYTES = 56 * 1024 * 1024

DA_HEADS = 4
DA_HEAD_DIM = 64
DA_V_DIM = 2 * DA_HEAD_DIM
DA_WIDTH = DA_HEADS * DA_V_DIM
RW_HEAD = 64
DECAY_LORA = 64
AAA_LORA = 64
GATE_LORA = 128
ROPE_THETA = 10000.0
N_EXPERTS = 32
TOP_K = 4
SWIGLU_ALPHA = 1.702
SWIGLU_LIMIT = 7.0
NORM_EPS = 1e-5
GN_EPS = 64e-5
LAYER_INDEX = 0
LAM_INIT = 0.8 - 0.6 * math.exp(-0.3 * LAYER_INDEX)


def _cparams(*semantics):
    return pltpu.CompilerParams(dimension_semantics=semantics, vmem_limit_bytes=VMEM_LIMIT_BYTES)


def _pick_tile(n, want):
    t = min(n, want)
    while n % t:
        t //= 2
    return t


def _inproj_body(x_ref, nw_ref, w_ref, cos_ref, sin_ref, q_ref, k_ref, v_ref, prw_ref):
    x = x_ref[...]
    ms = jnp.mean(x * x, axis=-1, keepdims=True)
    h = (x * lax.rsqrt(ms + NORM_EPS)) * nw_ref[...]
    p = jnp.dot(h.astype(BF16), w_ref[...], preferred_element_type=F32)
    cos = cos_ref[...]
    sin = sin_ref[...]
    lane = lax.broadcasted_iota(jnp.int32, cos.shape, 1)
    first_half = (lane % DA_HEAD_DIM) < (DA_HEAD_DIM // 2)
    scale = DA_HEAD_DIM ** -0.5

    def rope(t):
        partner = jnp.where(first_half,
                            pltpu.roll(t, V7X_LANES - DA_HEAD_DIM // 2, 1),
                            pltpu.roll(t, DA_HEAD_DIM // 2, 1))
        return t * cos + partner * sin

    for hh in range(DA_HEADS):
        sl = slice(hh * DA_V_DIM, (hh + 1) * DA_V_DIM)
        q_ref[:, sl] = (rope(p[:, sl]) * scale).astype(q_ref.dtype)
        ksl = slice(DA_WIDTH + hh * DA_V_DIM, DA_WIDTH + (hh + 1) * DA_V_DIM)
        k_ref[:, sl] = rope(p[:, ksl]).astype(k_ref.dtype)
    v_ref[...] = p[:, 2 * DA_WIDTH:3 * DA_WIDTH].astype(v_ref.dtype)
    prw_ref[...] = p[:, 3 * DA_WIDTH:]


def _input_projection(x2, norm_w, w_in, cos_t, sin_t):
    T, D = x2.shape
    n_cols = w_in.shape[1]
    rw_cols = n_cols - 3 * DA_WIDTH
    tm = _pick_tile(T, 512)
    row = lambda i: (i, 0)
    fixed = lambda i: (0, 0)
    return pl.pallas_call(
        _inproj_body,
        grid=(T // tm,),
        in_specs=[pl.BlockSpec((tm, D), row),
                  pl.BlockSpec((1, D), fixed),
                  pl.BlockSpec((D, n_cols), fixed),
                  pl.BlockSpec((tm, DA_V_DIM), row),
                  pl.BlockSpec((tm, DA_V_DIM), row)],
        out_specs=[pl.BlockSpec((tm, DA_WIDTH), row),
                   pl.BlockSpec((tm, DA_WIDTH), row),
                   pl.BlockSpec((tm, DA_WIDTH), row),
                   pl.BlockSpec((tm, rw_cols), row)],
        out_shape=[jax.ShapeDtypeStruct((T, DA_WIDTH), BF16),
                   jax.ShapeDtypeStruct((T, DA_WIDTH), BF16),
                   jax.ShapeDtypeStruct((T, DA_WIDTH), BF16),
                   jax.ShapeDtypeStruct((T, rw_cols), F32)],
        compiler_params=_cparams("parallel"),
        name="input_projection",
    )(x2, norm_w.reshape(1, D), w_in.astype(BF16), cos_t, sin_t)


def _rope_tables(positions):
    half = DA_HEAD_DIM // 2
    inv = ROPE_THETA ** (-jnp.arange(0, DA_HEAD_DIM, 2, dtype=F32) / DA_HEAD_DIM)
    ang = positions.reshape(-1).astype(F32)[:, None] * inv
    cos, sin = jnp.cos(ang), jnp.sin(ang)
    cos_t = jnp.tile(cos, (1, DA_V_DIM // half))
    sin_t = jnp.tile(jnp.concatenate([-sin, sin], axis=-1), (1, DA_V_DIM // DA_HEAD_DIM))
    return cos_t, sin_t


ATTN_Q_BLOCK = 1024
ATTN_K_BLOCK = 1024
ATTN_Q_SUB = 128


def _diff_attn_body(q_ref, k_ref, v_ref, lq1_ref, lk1_ref, lq2_ref, lk2_ref, sw_ref, o_ref,
                    m_ref, acc_ref, *, tq, tk, q_sub):
    qi = pl.program_id(2)
    q = q_ref[0]
    lane = lax.broadcasted_iota(jnp.int32, q.shape, 1)
    zero = jnp.zeros_like(q)
    q_heads = (jnp.where(lane < DA_HEAD_DIM, q, zero), jnp.where(lane >= DA_HEAD_DIM, q, zero))

    m_ref[...] = jnp.full(m_ref.shape, -jnp.inf, F32)
    acc_ref[...] = jnp.zeros(acc_ref.shape, F32)

    def step(j, diag):
        ks = pl.multiple_of(j * tk, tk)
        kb = k_ref[0, pl.ds(ks, tk), :]
        vb = v_ref[0, pl.ds(ks, tk), :]
        v_ext = jnp.concatenate([vb, jnp.ones_like(vb)], axis=1)
        k0 = 0 if diag is None else diag * tk
        units = []
        for r0 in range(0, tq, q_sub):
            n_keys, masked = tk, False
            if diag is not None:
                n_keys = min(tk, r0 + q_sub - k0)
                if n_keys <= 0:
                    continue
                masked = k0 + n_keys - 1 > r0
            units += [(c, r0, n_keys, masked) for c in range(2)]
        scores = [lax.dot_general(q_heads[c][r0:r0 + q_sub], kb[:n_keys], (((1,), (1,)), ((), ())),
                                  preferred_element_type=F32) for c, r0, n_keys, _ in units]
        probs, alphas = [], []
        for (c, r0, n_keys, masked), s in zip(units, scores):
            if masked:
                ri = lax.broadcasted_iota(jnp.int32, s.shape, 0) + r0
                ci = lax.broadcasted_iota(jnp.int32, s.shape, 1) + k0
                s = jnp.where(ci <= ri, s, -jnp.inf)
            m_prev = m_ref[c, r0:r0 + q_sub, :]
            m_new = jnp.maximum(m_prev, jnp.max(s, axis=-1, keepdims=True))
            m_ref[c, r0:r0 + q_sub, :] = m_new
            alphas.append(jnp.exp(m_prev - m_new))
            probs.append(jnp.concatenate(
                [jnp.exp(s[:, t * V7X_LANES:(t + 1) * V7X_LANES] - m_new).astype(vb.dtype)
                 for t in range(n_keys // V7X_LANES)], axis=1))
        for (c, r0, n_keys, _), p, alpha in zip(units, probs, alphas):
            pv = jnp.dot(p, v_ext[:n_keys], preferred_element_type=F32)
            rows = slice(r0, r0 + q_sub)
            acc_ref[c, rows, :] = jnp.concatenate([alpha, alpha], axis=1) * acc_ref[c, rows, :] + pv

    def full_step(j, carry):
        step(j, None)
        return carry

    lax.fori_loop(0, qi * (tq // tk), full_step, 0)
    for d in range(tq // tk):
        step(qi * (tq // tk) + d, d)

    lam = (jnp.exp(jnp.sum(lq1_ref[...] * lk1_ref[...], axis=-1, keepdims=True))
           - jnp.exp(jnp.sum(lq2_ref[...] * lk2_ref[...], axis=-1, keepdims=True)) + LAM_INIT)
    a0, a1 = acc_ref[0], acc_ref[1]
    o = a0[:, :DA_V_DIM] / a0[:, DA_V_DIM:] - lam * (a1[:, :DA_V_DIM] / a1[:, DA_V_DIM:])
    o = o * lax.rsqrt(jnp.mean(o * o, axis=-1, keepdims=True) + NORM_EPS)
    o_ref[0] = (o * sw_ref[...] * (1.0 - LAM_INIT)).astype(o_ref.dtype)


def _diff_attention(q, k, v, lq1, lk1, lq2, lk2, subln_w, B, S):
    tq = _pick_tile(S, ATTN_Q_BLOCK)
    tk = _pick_tile(tq, ATTN_K_BLOCK)
    q_sub = _pick_tile(tk, ATTN_Q_SUB)
    q3, k3, v3 = (t.reshape(B, S, DA_WIDTH) for t in (q, k, v))
    vec = lambda a: a.reshape(1, -1).astype(F32)
    blk = lambda b, h, i: (b, i, h)
    seq = lambda b, h, i: (b, 0, h)
    fixed = lambda b, h, i: (0, 0)
    out = pl.pallas_call(
        functools.partial(_diff_attn_body, tq=tq, tk=tk, q_sub=q_sub),
        grid=(B, DA_HEADS, S // tq),
        in_specs=[pl.BlockSpec((1, tq, DA_V_DIM), blk),
                  pl.BlockSpec((1, S, DA_V_DIM), seq),
                  pl.BlockSpec((1, S, DA_V_DIM), seq),
                  pl.BlockSpec((1, DA_HEAD_DIM), fixed),
                  pl.BlockSpec((1, DA_HEAD_DIM), fixed),
                  pl.BlockSpec((1, DA_HEAD_DIM), fixed),
                  pl.BlockSpec((1, DA_HEAD_DIM), fixed),
                  pl.BlockSpec((1, DA_V_DIM), fixed)],
        out_specs=pl.BlockSpec((1, tq, DA_V_DIM), blk),
        out_shape=jax.ShapeDtypeStruct((B, S, DA_WIDTH), BF16),
        scratch_shapes=[pltpu.VMEM((2, tq, V7X_LANES), F32),
                        pltpu.VMEM((2, tq, 2 * DA_V_DIM), F32)],
        compiler_params=_cparams("parallel", "parallel", "arbitrary"),
        name="diff_attention",
    )(q3, k3, v3, vec(lq1), vec(lk1), vec(lq2), vec(lk2), vec(subln_w))
    return out.reshape(B * S, DA_WIDTH)


RW_CHUNK = 64
RW_BLOCK = 256
RW_PAIR = 2 * RW_HEAD
RW_INV_DOUBLINGS = 5


def _bdot(a, b):
    return jnp.dot(a.astype(BF16), b.astype(BF16), preferred_element_type=F32)


def _bdot_nt(a, b):
    return lax.dot_general(a.astype(BF16), b.astype(BF16), (((1,), (1,)), ((), ())),
                           preferred_element_type=F32)


def _bdot_tn(a, b):
    return lax.dot_general(a.astype(BF16), b.astype(BF16), (((0,), (0,)), ((), ())),
                           preferred_element_type=F32)


def _split3_dot(m, x):
    mb = m.astype(BF16)
    x0 = x.astype(BF16)
    r1 = x - x0.astype(F32)
    x1 = r1.astype(BF16)
    x2 = (r1 - x1.astype(F32)).astype(BF16)
    dot = lambda t: jnp.dot(mb, t, preferred_element_type=F32)
    return dot(x0) + dot(x1) + dot(x2)


def _rwkv_body(p_ref, prev_ref, mu_ref, w0_ref, a0_ref, kk_ref, ka_ref, rk_ref, lnw_ref, lnb_ref,
               wa_ref, g2_ref, o_ref,
               s_ref, rt_ref, kh_ref, at_ref, bh_ref, v_ref, gl_ref, bonus_ref, gate_ref, y_ref, *, tb, width):
    i = pl.program_id(1)
    n_pairs = width // RW_PAIR
    C = RW_CHUNK

    @pl.when(i == 0)
    def _():
        s_ref[...] = jnp.zeros(s_ref.shape, F32)

    hr = lax.broadcasted_iota(jnp.int32, (width, width), 0) // RW_HEAD
    hc = lax.broadcasted_iota(jnp.int32, (width, width), 1) // RW_HEAD
    head_ones = jnp.where(hr == hc, 1.0, 0.0).astype(BF16)
    head_sum = lambda t: jnp.dot(t.astype(BF16), head_ones, preferred_element_type=F32)

    def stage_block():
        p = p_ref[0]
        row = lax.broadcasted_iota(jnp.int32, p.shape, 0)
        last_prev = jnp.where(i == 0, 0.0, prev_ref[0, 7:8, :])
        prev = jnp.where(row == 0, last_prev, pltpu.roll(p, 1, 0))
        ps = p + mu_ref[...] * (prev - p)
        r = ps[:, 0:width]
        k = ps[:, width:2 * width]
        v = ps[:, 2 * width:3 * width]
        c3 = 3 * width
        wa_in = ps[:, c3:c3 + DECAY_LORA + AAA_LORA]
        lane = lax.broadcasted_iota(jnp.int32, wa_in.shape, 1)
        wa_in = jnp.where(lane < DECAY_LORA, jnp.tanh(wa_in), wa_in)
        wa = _bdot(wa_in, wa_ref[...])
        gd = ps[:, c3 + DECAY_LORA + AAA_LORA:]
        g = _bdot(jax.nn.sigmoid(gd), g2_ref[...])
        wl = w0_ref[...] + wa[:, :width]
        softplus_neg = jnp.maximum(-wl, 0.0) + jnp.log1p(jnp.exp(-jnp.abs(wl)))
        logw = -jnp.exp(-softplus_neg - 0.5)
        a = jax.nn.sigmoid(a0_ref[...] + wa[:, width:])
        kk = k * kk_ref[...]
        kk = kk / jnp.maximum(jnp.sqrt(head_sum(kk * kk)), 1e-12)
        k2 = k * (1.0 + (a - 1.0) * ka_ref[...])
        bonus = head_sum(r * k2 * rk_ref[...]) * v
        tr = lax.broadcasted_iota(jnp.int32, (tb, tb), 0)
        tc = lax.broadcasted_iota(jnp.int32, (tb, tb), 1)
        chunk_tril = jnp.where((tr // C == tc // C) & (tc <= tr), 1.0, 0.0)
        cum = _split3_dot(chunk_tril, logw)
        g_inc = jnp.exp(cum)
        g_inv = jnp.exp(-cum)
        g_exc = jnp.exp(cum - logw)
        rt_ref[...] = r * g_inc
        kh_ref[...] = k2 * g_inv
        at_ref[...] = -kk * g_exc
        bh_ref[...] = kk * a * g_inv
        v_ref[...] = v
        gl_ref[...] = g_inc
        bonus_ref[...] = bonus
        gate_ref[...] = g

    stage_block()

    bonus_prev = bonus_ref[...]
    gate_prev = gate_ref[...]
    lane_p = lax.broadcasted_iota(jnp.int32, (C, RW_PAIR), 1)
    first = lane_p < RW_HEAD
    ri = lax.broadcasted_iota(jnp.int32, (RW_PAIR, RW_PAIR), 0)
    ci = lax.broadcasted_iota(jnp.int32, (RW_PAIR, RW_PAIR), 1)
    strict = ci < ri
    incl = ci <= ri
    eye = jnp.where(ci == ri, 1.0, 0.0)

    def stack(t):
        return jnp.concatenate([jnp.where(first, t, 0.0), jnp.where(first, 0.0, t)], axis=0)

    n_chunks = tb // C
    units = [(c, j) for c in range(n_chunks) for j in range(n_pairs)]
    U = range(len(units))
    rows = [slice(c * C, (c + 1) * C) for c, _ in units]
    lanes = [slice(j * RW_PAIR, (j + 1) * RW_PAIR) for _, j in units]
    rts = [stack(rt_ref[rows[n], lanes[n]]) for n in U]
    khs = [stack(kh_ref[rows[n], lanes[n]]) for n in U]
    ats = [stack(at_ref[rows[n], lanes[n]]) for n in U]
    bhs = [stack(bh_ref[rows[n], lanes[n]]) for n in U]
    vs = [stack(v_ref[rows[n], lanes[n]]) for n in U]
    g_last = [gl_ref[(c + 1) * C - 1:(c + 1) * C, lanes[n]] for n, (c, _) in enumerate(units)]

    sc = [_bdot_nt(jnp.concatenate([ats[n], rts[n]], axis=0), jnp.concatenate([bhs[n], khs[n]], axis=0))
          for n in U]
    l_ab = [jnp.where(strict, sc[n][:RW_PAIR, :RW_PAIR], 0.0) for n in U]
    l_ak = [jnp.where(strict, sc[n][:RW_PAIR, RW_PAIR:], 0.0) for n in U]
    a_rb = [jnp.where(incl, sc[n][RW_PAIR:, :RW_PAIR], 0.0) for n in U]
    a_rk = [jnp.where(incl, sc[n][RW_PAIR:, RW_PAIR:], 0.0) for n in U]

    pw = [_bdot(l_ab[n], l_ab[n]) for n in U]
    tinv = [eye + l_ab[n] for n in U]
    for _ in range(1, RW_INV_DOUBLINGS):
        both = [_bdot(jnp.concatenate([pw[n], tinv[n]], axis=0), pw[n]) for n in U]
        tinv = [tinv[n] + both[n][RW_PAIR:] for n in U]
        pw = [both[n][:RW_PAIR] for n in U]
    tinv = [tinv[n] + _bdot(tinv[n], pw[n]) for n in U]

    lv = [_bdot(l_ak[n], vs[n]) for n in U]
    g0 = [_bdot_tn(vs[n], khs[n]) for n in U]
    taw = [_bdot(tinv[n], jnp.concatenate([ats[n], lv[n]], axis=1)) for n in U]
    ar = [_bdot(jnp.concatenate([a_rb[n], a_rk[n]], axis=1),
                jnp.concatenate([taw[n], jnp.concatenate([jnp.zeros_like(vs[n]), vs[n]], axis=1)], axis=0))
          for n in U]
    qt_ta = [jnp.concatenate([rts[n] + ar[n][:, :RW_PAIR], taw[n][:, :RW_PAIR]], axis=0) for n in U]

    state = [s_ref[j] for j in range(n_pairs)]
    for c in range(n_chunks):
        ns = [c * n_pairs + j for j in range(n_pairs)]
        m1 = [_bdot_nt(qt_ta[n], state[j]) for j, n in enumerate(ns)]
        u = [m1[j][RW_PAIR:] + taw[n][:, RW_PAIR:] for j, n in enumerate(ns)]
        state = [(state[j] + g0[n] + _bdot_tn(u[j], bhs[n])) * g_last[n] for j, n in enumerate(ns)]
        for j, n in enumerate(ns):
            y = m1[j][:RW_PAIR] + ar[n][:, RW_PAIR:]
            y_ref[rows[n], lanes[n]] = y[:C] + y[C:]
    for j in range(n_pairs):
        s_ref[j] = state[j]

    y = y_ref[...]
    inv_n = 1.0 / RW_HEAD
    mean = head_sum(y) * inv_n
    yc = y - mean
    var = head_sum(yc * yc) * inv_n
    yn = yc * lax.rsqrt(var + GN_EPS) * lnw_ref[...] + lnb_ref[...]
    o_ref[0] = ((yn + bonus_prev) * gate_prev).astype(o_ref.dtype)


def _rwkv7(p_rw, shift_mu, w0, w2, a0, a2, g2, k_k, k_a, r_k, lnx_w, lnx_b, B, S):
    cols = p_rw.shape[-1]
    width = (cols - DECAY_LORA - AAA_LORA - GATE_LORA) // 3
    tb = _pick_tile(S, RW_BLOCK)
    assert tb % RW_CHUNK == 0 and width % RW_PAIR == 0
    p3 = p_rw.reshape(B, S, cols)
    vec = lambda t: t.reshape(1, -1).astype(F32)
    zeros = jnp.zeros_like(w2)
    wa = jnp.concatenate([jnp.concatenate([w2, zeros], axis=1),
                          jnp.concatenate([jnp.zeros_like(a2), a2], axis=1)], axis=0).astype(BF16)
    blk = lambda b, i: (b, i, 0)
    prev_blk = lambda b, i: (b, jnp.maximum(i * (tb // 8) - 1, 0), 0)
    fixed = lambda b, i: (0, 0)
    wide = pltpu.VMEM((tb, width), F32)
    out = pl.pallas_call(
        functools.partial(_rwkv_body, tb=tb, width=width),
        grid=(B, S // tb),
        in_specs=[pl.BlockSpec((1, tb, cols), blk),
                  pl.BlockSpec((1, 8, cols), prev_blk),
                  pl.BlockSpec((1, cols), fixed)]
                 + [pl.BlockSpec((1, width), fixed)] * 7
                 + [pl.BlockSpec((DECAY_LORA + AAA_LORA, 2 * width), fixed),
                    pl.BlockSpec((GATE_LORA, width), fixed)],
        out_specs=pl.BlockSpec((1, tb, width), blk),
        out_shape=jax.ShapeDtypeStruct((B, S, width), BF16),
        scratch_shapes=[pltpu.VMEM((width // RW_PAIR, RW_PAIR, RW_PAIR), F32)] + [wide] * 9,
        compiler_params=_cparams("parallel", "arbitrary"),
        name="rwkv7_mix",
    )(p3, p3, vec(shift_mu), vec(w0), vec(a0), vec(k_k), vec(k_a), vec(r_k), vec(lnx_w), vec(lnx_b),
      wa, g2.astype(BF16))
    return out.reshape(B * S, width)


def _outproj_router_body(oda_ref, orw_ref, x_ref, wa_ref, wb_ref, nw_ref, rw_ref, rb_ref,
                         x1_ref, h_ref, idx_ref, gate_ref, rank_ref, cnt_ref, carry_ref):
    i = pl.program_id(0)

    @pl.when(i == 0)
    def _():
        carry_ref[...] = jnp.zeros(carry_ref.shape, F32)

    x1 = (x_ref[...] + jnp.dot(oda_ref[...], wa_ref[...], preferred_element_type=F32)
          + jnp.dot(orw_ref[...], wb_ref[...], preferred_element_type=F32))
    x1_ref[...] = x1
    h = (x1 * lax.rsqrt(jnp.mean(x1 * x1, axis=-1, keepdims=True) + NORM_EPS)) * nw_ref[...]
    h_ref[...] = h
    h_hi = h.astype(BF16)
    h_lo = (h - h_hi.astype(F32)).astype(BF16)
    rw = rw_ref[...]
    rw_hi = rw.astype(BF16)
    rw_lo = (rw - rw_hi.astype(F32)).astype(BF16)
    logits = (jnp.dot(h_hi, rw_hi, preferred_element_type=F32) + jnp.dot(h_lo, rw_hi, preferred_element_type=F32)
              + jnp.dot(h_hi, rw_lo, preferred_element_type=F32) + rb_ref[...])
    tm, n_exp = logits.shape
    lane = lax.broadcasted_iota(jnp.int32, logits.shape, 1).astype(F32)
    out_lane = lax.broadcasted_iota(jnp.int32, (tm, TOP_K), 1)

    vals, idxs = [], []
    rest = logits
    for _ in range(TOP_K):
        m = jnp.max(rest, axis=-1, keepdims=True)
        sel = jnp.min(jnp.where(rest == m, lane, float(n_exp)), axis=-1, keepdims=True)
        vals.append(m)
        idxs.append(sel)
        rest = jnp.where(lane == sel, -jnp.inf, rest)
    exps = [jnp.exp(v - vals[0]) for v in vals]
    denom = exps[0]
    for e in exps[1:]:
        denom = denom + e

    hot = [jnp.where(lane == s, 1.0, 0.0) for s in idxs]
    hot_all = hot[0]
    for t in hot[1:]:
        hot_all = hot_all + t
    tr = lax.broadcasted_iota(jnp.int32, (tm, tm), 0)
    tc = lax.broadcasted_iota(jnp.int32, (tm, tm), 1)
    before = jnp.dot(jnp.where(tc < tr, 1.0, 0.0).astype(BF16), hot_all.astype(BF16),
                     preferred_element_type=F32) + carry_ref[...]

    idx_out = jnp.zeros((tm, TOP_K), F32)
    gate_out = jnp.zeros((tm, TOP_K), F32)
    rank_out = jnp.zeros((tm, TOP_K), F32)
    for kk in range(TOP_K):
        rank_k = jnp.sum(hot[kk] * before, axis=-1, keepdims=True)
        idx_out = jnp.where(out_lane == kk, idxs[kk], idx_out)
        gate_out = jnp.where(out_lane == kk, exps[kk] / denom, gate_out)
        rank_out = jnp.where(out_lane == kk, rank_k, rank_out)
    idx_ref[...] = idx_out.astype(jnp.int32)
    gate_ref[...] = gate_out
    rank_ref[...] = rank_out.astype(jnp.int32)
    carry_ref[...] += jnp.sum(hot_all, axis=0, keepdims=True)
    cnt_ref[...] = carry_ref[...]


def _outproj_router(o_da, o_rw, x2, w_out, ffn_norm_w, router_w, router_b):
    T, D = x2.shape
    n_exp = router_w.shape[1]
    tm = _pick_tile(T, 512)
    wa = w_out[:DA_WIDTH].astype(BF16)
    wb = w_out[DA_WIDTH:].astype(BF16)
    row = lambda i: (i, 0)
    fixed = lambda i: (0, 0)
    return pl.pallas_call(
        _outproj_router_body,
        grid=(T // tm,),
        in_specs=[pl.BlockSpec((tm, DA_WIDTH), row),
                  pl.BlockSpec((tm, o_rw.shape[1]), row),
                  pl.BlockSpec((tm, D), row),
                  pl.BlockSpec(wa.shape, fixed),
                  pl.BlockSpec(wb.shape, fixed),
                  pl.BlockSpec((1, D), fixed),
                  pl.BlockSpec((D, n_exp), fixed),
                  pl.BlockSpec((1, n_exp), fixed)],
        out_specs=[pl.BlockSpec((tm, D), row),
                   pl.BlockSpec((tm, D), row),
                   pl.BlockSpec((tm, TOP_K), row),
                   pl.BlockSpec((tm, TOP_K), row),
                   pl.BlockSpec((tm, TOP_K), row),
                   pl.BlockSpec((1, n_exp), fixed)],
        out_shape=[jax.ShapeDtypeStruct((T, D), F32),
                   jax.ShapeDtypeStruct((T, D), F32),
                   jax.ShapeDtypeStruct((T, TOP_K), jnp.int32),
                   jax.ShapeDtypeStruct((T, TOP_K), F32),
                   jax.ShapeDtypeStruct((T, TOP_K), jnp.int32),
                   jax.ShapeDtypeStruct((1, n_exp), F32)],
        scratch_shapes=[pltpu.VMEM((1, n_exp), F32)],
        compiler_params=_cparams("arbitrary"),
        name="outproj_router",
    )(o_da, o_rw, x2, wa, wb, ffn_norm_w.reshape(1, D), router_w, router_b.reshape(1, n_exp))


MOE_ROWS = 512
DISPATCH_TOKENS = 256
ISSUE_UNROLL = 8
MXU_TILE = 256
SUBLANES = 8


def _store_row_tiled(ref, x, lead=(), first_row=0):
    n = x.shape[0]
    for c in range(x.shape[1] // V7X_LANES):
        rows = pl.ds(first_row * SUBLANES + c, n, stride=SUBLANES)
        ref[lead + (rows, slice(None))] = x[:, c * V7X_LANES:(c + 1) * V7X_LANES]


def _load_row_tiled(ref, n, c, lead=(), first_row=0):
    return ref[lead + (pl.ds(first_row * SUBLANES + c, n, stride=SUBLANES), slice(None))]


def _tile_rows(r):
    return pl.multiple_of(r * SUBLANES, SUBLANES)


def _regroup_expert_weights(w1_ref, w2_ref, o1_ref, o2_ref):
    half = MXU_TILE // 2
    r = lax.broadcasted_iota(jnp.int32, (MXU_TILE, MXU_TILE), 0)
    c = lax.broadcasted_iota(jnp.int32, (MXU_TILE, MXU_TILE), 1)
    perm = jnp.where(c == (r >> 1) + (r & 1) * half, 1.0, 0.0).astype(BF16)
    for g in range(w1_ref.shape[2] // MXU_TILE):
        sl = slice(g * MXU_TILE, (g + 1) * MXU_TILE)
        o1_ref[0, :, sl] = jnp.dot(w1_ref[0, :, sl].astype(BF16), perm,
                                   preferred_element_type=F32).astype(o1_ref.dtype)
    o2_ref[...] = w2_ref[...].astype(o2_ref.dtype)


def _dispatch_body(pad_ref, dest_ref, h_ref, w1_ref, w2_ref, xs_hbm, o1_ref, o2_ref,
                   stage_ref, zero_ref, sems, zero_sem, *, steps_per_expert):
    i = pl.program_id(0)
    n_tok = h_ref.shape[0]

    @pl.when(i == 0)
    def _():
        zero_ref[...] = jnp.zeros(zero_ref.shape, zero_ref.dtype)

        def fill(start_row):
            cp = pltpu.make_async_copy(zero_ref, xs_hbm.at[pl.ds(_tile_rows(start_row), zero_ref.shape[0])],
                                       zero_sem)
            cp.start()
            cp.wait()

        def fill_pad(e, carry):
            fill(pad_ref[e])
            return carry

        def fill_tail(b, carry):
            fill(b * MOE_ROWS)
            return carry

        n_exp = pad_ref.shape[0] - 1
        lax.fori_loop(0, n_exp, fill_pad, 0)
        lax.fori_loop(pad_ref[n_exp], xs_hbm.shape[0] // zero_ref.shape[0], fill_tail, 0)

    slot = i % 2
    _store_row_tiled(stage_ref, h_ref[...], lead=(slot,))

    def issue(t, carry):
        src = stage_ref.at[slot, pl.ds(_tile_rows(t), SUBLANES)]
        for kk in range(TOP_K):
            dst = xs_hbm.at[pl.ds(_tile_rows(dest_ref[0, 0, t * TOP_K + kk]), SUBLANES)]
            pltpu.make_async_copy(src, dst, sems.at[slot]).start()
        return carry

    lax.fori_loop(0, n_tok, issue, 0, unroll=ISSUE_UNROLL)

    @pl.when(i % steps_per_expert == 0)
    def _():
        _regroup_expert_weights(w1_ref, w2_ref, o1_ref, o2_ref)

    def drain(s):
        for _ in range(TOP_K):
            pltpu.make_async_copy(stage_ref.at[s], xs_hbm.at[pl.ds(0, stage_ref.shape[1])], sems.at[s]).wait()

    @pl.when(i > 0)
    def _():
        drain(1 - slot)

    @pl.when(i == pl.num_programs(0) - 1)
    def _():
        drain(slot)


def _dispatch(h, dest, pad_start, n_padded, mlp1_w, mlp2_w):
    T, D = h.shape
    assert D == SUBLANES * V7X_LANES, "one logical row must be exactly one (8, 128) tile"
    n_exp = mlp1_w.shape[0]
    steps_per_expert = max(1, T // (n_exp * DISPATCH_TOKENS))
    n_steps = n_exp * steps_per_expert
    n_tok = T // n_steps
    assert n_tok * n_steps == T and n_tok % ISSUE_UNROLL == 0
    dest3 = dest.reshape(n_steps, 1, n_tok * TOP_K)
    wsel = lambda i, pad: (i // steps_per_expert, 0, 0)
    spec1 = pl.BlockSpec((1,) + mlp1_w.shape[1:], wsel)
    spec2 = pl.BlockSpec((1,) + mlp2_w.shape[1:], wsel)
    return pl.pallas_call(
        functools.partial(_dispatch_body, steps_per_expert=steps_per_expert),
        grid_spec=pltpu.PrefetchScalarGridSpec(
            num_scalar_prefetch=1,
            grid=(n_steps,),
            in_specs=[pl.BlockSpec((1, 1, n_tok * TOP_K), lambda i, pad: (i, 0, 0), memory_space=pltpu.SMEM),
                      pl.BlockSpec((n_tok, D), lambda i, pad: (i, 0)),
                      spec1, spec2],
            out_specs=[pl.BlockSpec(memory_space=pl.ANY), spec1, spec2],
            scratch_shapes=[pltpu.VMEM((2, n_tok * SUBLANES, V7X_LANES), h.dtype),
                            pltpu.VMEM((MOE_ROWS * SUBLANES, V7X_LANES), h.dtype),
                            pltpu.SemaphoreType.DMA((2,)),
                            pltpu.SemaphoreType.DMA(())]),
        out_shape=[jax.ShapeDtypeStruct((n_padded * SUBLANES, V7X_LANES), h.dtype),
                   jax.ShapeDtypeStruct(mlp1_w.shape, BF16),
                   jax.ShapeDtypeStruct(mlp2_w.shape, BF16)],
        compiler_params=_cparams("arbitrary"),
        name="moe_dispatch",
    )(pad_start, dest3, h, mlp1_w, mlp2_w)


def _experts_body(be_ref, nu_ref, xs_ref, w1_ref, b1_ref, w2_ref, b2_ref, y_ref):
    del be_ref
    i = pl.program_id(0)
    half = MXU_TILE // 2

    @pl.when(i < nu_ref[0])
    def _():
        n_rows = xs_ref.shape[0] // SUBLANES
        x = jnp.concatenate([_load_row_tiled(xs_ref, n_rows, c).astype(BF16) for c in range(SUBLANES)], axis=1)
        gu = jnp.dot(x, w1_ref[0], preferred_element_type=F32) + b1_ref[0]
        acts = []
        for g in range(gu.shape[1] // MXU_TILE):
            gate = jnp.minimum(gu[:, g * MXU_TILE:g * MXU_TILE + half], SWIGLU_LIMIT)
            up = jnp.clip(gu[:, g * MXU_TILE + half:(g + 1) * MXU_TILE], -SWIGLU_LIMIT, SWIGLU_LIMIT)
            acts.append(((up + 1.0) * gate * jax.nn.sigmoid(SWIGLU_ALPHA * gate)).astype(BF16))
        act = jnp.concatenate(acts, axis=1)
        _store_row_tiled(y_ref, jnp.dot(act, w2_ref[0], preferred_element_type=F32) + b2_ref[0])

    @pl.when(i >= nu_ref[0])
    def _():
        y_ref[...] = jnp.zeros(y_ref.shape, y_ref.dtype)


def _experts(xs, blk_expert, n_used, w1, b1, w2, b2):
    blk_rows = MOE_ROWS * SUBLANES
    n_blocks = xs.shape[0] // blk_rows
    D, gu_cols = w1.shape[1], w1.shape[2]
    xrow = lambda i, be, nu: (jnp.maximum(jnp.minimum(i, nu[0] - 1), 0), 0)
    yrow = lambda i, be, nu: (i, 0)
    wsel = lambda i, be, nu: (be[i], 0, 0)
    return pl.pallas_call(
        _experts_body,
        grid_spec=pltpu.PrefetchScalarGridSpec(
            num_scalar_prefetch=2,
            grid=(n_blocks,),
            in_specs=[pl.BlockSpec((blk_rows, V7X_LANES), xrow),
                      pl.BlockSpec((1, D, gu_cols), wsel),
                      pl.BlockSpec((1, 1, gu_cols), wsel),
                      pl.BlockSpec((1, gu_cols // 2, D), wsel),
                      pl.BlockSpec((1, 1, D), wsel)],
            out_specs=pl.BlockSpec((blk_rows, V7X_LANES), yrow)),
        out_shape=jax.ShapeDtypeStruct(xs.shape, F32),
        compiler_params=_cparams("arbitrary"),
        name="moe_experts",
    )(blk_expert, n_used, xs, w1, b1, w2, b2)


def _combine_body(dest_ref, next_ref, ys_hbm, x1_ref, gate_ref, fw_ref, o_ref, buf_ref, sems):
    i = pl.program_id(0)
    slot = i % 2

    def gather(d_ref, s):
        def issue(t, carry):
            for kk in range(TOP_K):
                src = ys_hbm.at[pl.ds(_tile_rows(d_ref[0, 0, t * TOP_K + kk]), SUBLANES)]
                pltpu.make_async_copy(src, buf_ref.at[s, kk, pl.ds(_tile_rows(t), SUBLANES)], sems.at[s]).start()
            return carry

        lax.fori_loop(0, DISPATCH_TOKENS, issue, 0, unroll=ISSUE_UNROLL)

    @pl.when(i == 0)
    def _():
        gather(dest_ref, 0)

    @pl.when(i + 1 < pl.num_programs(0))
    def _():
        gather(next_ref, 1 - slot)

    for kk in range(TOP_K):
        pltpu.make_async_copy(ys_hbm.at[pl.ds(0, buf_ref.shape[2])], buf_ref.at[slot, kk], sems.at[slot]).wait()

    gates = gate_ref[...]
    pieces = []
    sq = None
    for c in range(SUBLANES):
        piece = x1_ref[:, c * V7X_LANES:(c + 1) * V7X_LANES]
        for kk in range(TOP_K):
            piece = piece + gates[:, kk:kk + 1] * _load_row_tiled(buf_ref, DISPATCH_TOKENS, c, lead=(slot, kk))
        pieces.append(piece)
        sq = piece * piece if sq is None else sq + piece * piece
    inv = lax.rsqrt(jnp.sum(sq, axis=-1, keepdims=True) * (1.0 / (SUBLANES * V7X_LANES)) + NORM_EPS)
    for c, piece in enumerate(pieces):
        lanes = slice(c * V7X_LANES, (c + 1) * V7X_LANES)
        o_ref[:, lanes] = piece * inv * fw_ref[:, lanes]


def _combine(ys, dest, x1, gates, final_w):
    T, D = x1.shape
    n_blocks = T // DISPATCH_TOKENS
    dest3 = dest.reshape(n_blocks, 1, DISPATCH_TOKENS * TOP_K)
    row = lambda i: (i, 0)
    idx_blk = (1, 1, DISPATCH_TOKENS * TOP_K)
    return pl.pallas_call(
        _combine_body,
        grid=(n_blocks,),
        in_specs=[pl.BlockSpec(idx_blk, lambda i: (i, 0, 0), memory_space=pltpu.SMEM),
                  pl.BlockSpec(idx_blk, lambda i: (jnp.minimum(i + 1, n_blocks - 1), 0, 0),
                               memory_space=pltpu.SMEM),
                  pl.BlockSpec(memory_space=pl.ANY),
                  pl.BlockSpec((DISPATCH_TOKENS, D), row),
                  pl.BlockSpec((DISPATCH_TOKENS, TOP_K), row),
                  pl.BlockSpec((1, D), lambda i: (0, 0))],
        out_specs=pl.BlockSpec((DISPATCH_TOKENS, D), row),
        out_shape=jax.ShapeDtypeStruct((T, D), F32),
        scratch_shapes=[pltpu.VMEM((2, TOP_K, DISPATCH_TOKENS * SUBLANES, V7X_LANES), F32),
                        pltpu.SemaphoreType.DMA((2,))],
        compiler_params=_cparams("arbitrary"),
        name="moe_combine",
    )(dest3, dest3, ys, x1, gates, final_w.reshape(1, D))


def _moe(h, x1, idx, gates, rank, counts, mlp1_w, mlp1_b, mlp2_w, mlp2_b, final_w):
    T, D = h.shape
    n_exp = mlp1_w.shape[0]
    assert T % DISPATCH_TOKENS == 0
    n_padded = T * TOP_K + n_exp * MOE_ROWS
    n_blocks = n_padded // MOE_ROWS
    counts = counts.reshape(n_exp).astype(jnp.int32)
    padded = (counts + MOE_ROWS - 1) // MOE_ROWS * MOE_ROWS
    pend = jnp.cumsum(padded)
    pstart = pend - padded
    dest = (pstart[idx] + rank).astype(jnp.int32)
    starts = jnp.arange(n_blocks, dtype=jnp.int32) * MOE_ROWS
    blk_expert = jnp.minimum(jnp.sum((pend[None, :] <= starts[:, None]).astype(jnp.int32), axis=1),
                             n_exp - 1).astype(jnp.int32)
    n_used = (pend[-1:] // MOE_ROWS).astype(jnp.int32)

    half = MXU_TILE // 2
    b1 = mlp1_b.reshape(n_exp, -1, half, 2).transpose(0, 1, 3, 2).reshape(n_exp, 1, -1)
    pad_start = jnp.minimum(pstart + counts, n_padded - MOE_ROWS).astype(jnp.int32)
    xs, w1, w2 = _dispatch(h, dest, jnp.concatenate([pad_start, n_used]), n_padded, mlp1_w, mlp2_w)
    ys = _experts(xs, blk_expert, n_used, w1, b1, w2, mlp2_b[:, None, :])
    return _combine(ys, dest, x1, gates, final_w)


def kernel(x, positions, attn_norm_w, w_in, shift_mu, lambda_q1, lambda_k1, lambda_q2, lambda_k2, subln_w,
           rw_w0, rw_w2, rw_a0, rw_a2, rw_g2, rw_k_k, rw_k_a, rw_r_k, rw_lnx_w, rw_lnx_b, w_out, ffn_norm_w,
           router_w, router_b, mlp1_w, mlp1_b, mlp2_w, mlp2_b, final_norm_w):
    B, S, D = x.shape
    assert attn_norm_w.shape[0] == 1, "single-layer stack"
    l = LAYER_INDEX
    x2 = x.reshape(B * S, D)
    cos_t, sin_t = _rope_tables(positions)
    q, k, v, p_rw = _input_projection(x2, attn_norm_w[l], w_in[l], cos_t, sin_t)
    o_da = _diff_attention(q, k, v, lambda_q1[l], lambda_k1[l], lambda_q2[l], lambda_k2[l], subln_w[l], B, S)
    o_rw = _rwkv7(p_rw, shift_mu[l], rw_w0[l], rw_w2[l], rw_a0[l], rw_a2[l], rw_g2[l],
                  rw_k_k[l], rw_k_a[l], rw_r_k[l], rw_lnx_w[l], rw_lnx_b[l], B, S)
    x1, h, idx, gates, rank, counts = _outproj_router(o_da, o_rw, x2, w_out[l], ffn_norm_w[l],
                                                      router_w[l], router_b[l])
    out = _moe(h, x1, idx, gates, rank, counts, mlp1_w[l], mlp1_b[l], mlp2_w[l], mlp2_b[l], final_norm_w)
    return out.reshape(B, S, D)
```

```python
import functools
import math

import jax
import jax.numpy as jnp
from jax import lax
from jax.experimental import pallas as pl
from jax.experimental.pallas import tpu as pltpu

F32 = jnp.float32
BF16 = jnp.bfloat16

V7X_LANES = 128
V7X_VMEM_BYTES = 64 * 1024 * 1024
VMEM_LIMIT_BYTES = V7X_VMEM_BYTES * 7 // 8
PROJ_ROWS = 1024

DA_HEADS = 4
DA_HEAD_DIM = 64
DA_V_DIM = 2 * DA_HEAD_DIM
DA_WIDTH = DA_HEADS * DA_V_DIM
RW_HEAD = 64
DECAY_LORA = 64
AAA_LORA = 64
GATE_LORA = 128
ROPE_THETA = 10000.0
TOP_K = 4
SWIGLU_ALPHA = 1.702
SWIGLU_LIMIT = 7.0
NORM_EPS = 1e-5
GN_EPS = 64e-5
LAYER_INDEX = 0
LAM_INIT = 0.8 - 0.6 * math.exp(-0.3 * LAYER_INDEX)


def _cparams(*semantics):
    return pltpu.CompilerParams(dimension_semantics=semantics, vmem_limit_bytes=VMEM_LIMIT_BYTES)


def _pick_tile(n, want):
    t = min(n, want)
    while n % t:
        t //= 2
    return t


def _inproj_body(x_ref, nw_ref, w_ref, cos_ref, sin_ref, q_ref, k_ref, v_ref, prw_ref):
    x = x_ref[...]
    ms = jnp.mean(x * x, axis=-1, keepdims=True)
    h = (x * lax.rsqrt(ms + NORM_EPS)) * nw_ref[...]
    p = jnp.dot(h.astype(BF16), w_ref[...], preferred_element_type=F32)
    cos = cos_ref[...]
    sin = sin_ref[...]
    lane = lax.broadcasted_iota(jnp.int32, cos.shape, 1)
    first_half = (lane % DA_HEAD_DIM) < (DA_HEAD_DIM // 2)
    scale = DA_HEAD_DIM ** -0.5

    def rope(t):
        partner = jnp.where(first_half,
                            pltpu.roll(t, V7X_LANES - DA_HEAD_DIM // 2, 1),
                            pltpu.roll(t, DA_HEAD_DIM // 2, 1))
        return t * cos + partner * sin

    for hh in range(DA_HEADS):
        sl = slice(hh * DA_V_DIM, (hh + 1) * DA_V_DIM)
        q_ref[:, sl] = (rope(p[:, sl]) * scale).astype(q_ref.dtype)
        ksl = slice(DA_WIDTH + hh * DA_V_DIM, DA_WIDTH + (hh + 1) * DA_V_DIM)
        k_ref[:, sl] = rope(p[:, ksl]).astype(k_ref.dtype)
    v_ref[...] = p[:, 2 * DA_WIDTH:3 * DA_WIDTH].astype(v_ref.dtype)
    prw_ref[...] = p[:, 3 * DA_WIDTH:]


def _input_projection(x2, norm_w, w_in, cos_t, sin_t):
    T, D = x2.shape
    n_cols = w_in.shape[1]
    rw_cols = n_cols - 3 * DA_WIDTH
    tm = _pick_tile(T, PROJ_ROWS)
    row = lambda i: (i, 0)
    fixed = lambda i: (0, 0)
    return pl.pallas_call(
        _inproj_body,
        grid=(T // tm,),
        in_specs=[pl.BlockSpec((tm, D), row),
                  pl.BlockSpec((1, D), fixed),
                  pl.BlockSpec((D, n_cols), fixed),
                  pl.BlockSpec((tm, DA_V_DIM), row),
                  pl.BlockSpec((tm, DA_V_DIM), row)],
        out_specs=[pl.BlockSpec((tm, DA_WIDTH), row),
                   pl.BlockSpec((tm, DA_WIDTH), row),
                   pl.BlockSpec((tm, DA_WIDTH), row),
                   pl.BlockSpec((tm, rw_cols), row)],
        out_shape=[jax.ShapeDtypeStruct((T, DA_WIDTH), BF16),
                   jax.ShapeDtypeStruct((T, DA_WIDTH), BF16),
                   jax.ShapeDtypeStruct((T, DA_WIDTH), BF16),
                   jax.ShapeDtypeStruct((T, rw_cols), F32)],
        compiler_params=_cparams("parallel"),
        name="input_projection",
    )(x2, norm_w.reshape(1, D), w_in.astype(BF16), cos_t, sin_t)


def _rope_tables(positions):
    half = DA_HEAD_DIM // 2
    inv = ROPE_THETA ** (-jnp.arange(0, DA_HEAD_DIM, 2, dtype=F32) / DA_HEAD_DIM)
    ang = positions.reshape(-1).astype(F32)[:, None] * inv
    cos, sin = jnp.cos(ang), jnp.sin(ang)
    cos_t = jnp.tile(cos, (1, DA_V_DIM // half))
    sin_t = jnp.tile(jnp.concatenate([-sin, sin], axis=-1), (1, DA_V_DIM // DA_HEAD_DIM))
    return cos_t, sin_t


ATTN_Q_BLOCK = 1024
ATTN_K_BLOCK = 1024
ATTN_Q_SUB = 128


def _diff_attn_body(q_ref, k_ref, v_ref, lq1_ref, lk1_ref, lq2_ref, lk2_ref, sw_ref, o_ref,
                    m_ref, acc_ref, *, tq, tk, q_sub):
    qi = pl.program_id(2)
    q = q_ref[0]
    lane = lax.broadcasted_iota(jnp.int32, q.shape, 1)
    zero = jnp.zeros_like(q)
    q_heads = (jnp.where(lane < DA_HEAD_DIM, q, zero), jnp.where(lane >= DA_HEAD_DIM, q, zero))

    m_ref[...] = jnp.full(m_ref.shape, -jnp.inf, F32)
    acc_ref[...] = jnp.zeros(acc_ref.shape, F32)

    def step(j, diag):
        ks = pl.multiple_of(j * tk, tk)
        kb = k_ref[0, pl.ds(ks, tk), :]
        vb = v_ref[0, pl.ds(ks, tk), :]
        v_ext = jnp.concatenate([vb, jnp.ones_like(vb)], axis=1)
        k0 = 0 if diag is None else diag * tk
        units = []
        for r0 in range(0, tq, q_sub):
            n_keys, masked = tk, False
            if diag is not None:
                n_keys = min(tk, r0 + q_sub - k0)
                if n_keys <= 0:
                    continue
                masked = k0 + n_keys - 1 > r0
            units += [(c, r0, n_keys, masked) for c in range(2)]
        scores = [lax.dot_general(q_heads[c][r0:r0 + q_sub], kb[:n_keys], (((1,), (1,)), ((), ())),
                                  preferred_element_type=F32) for c, r0, n_keys, _ in units]
        probs, alphas = [], []
        for (c, r0, n_keys, masked), s in zip(units, scores):
            if masked:
                ri = lax.broadcasted_iota(jnp.int32, s.shape, 0) + r0
                ci = lax.broadcasted_iota(jnp.int32, s.shape, 1) + k0
                s = jnp.where(ci <= ri, s, -jnp.inf)
            m_prev = m_ref[c, r0:r0 + q_sub, :]
            m_new = jnp.maximum(m_prev, jnp.max(s, axis=-1, keepdims=True))
            m_ref[c, r0:r0 + q_sub, :] = m_new
            alphas.append(jnp.exp(m_prev - m_new))
            probs.append(jnp.concatenate(
                [jnp.exp(s[:, t * V7X_LANES:(t + 1) * V7X_LANES] - m_new).astype(vb.dtype)
                 for t in range(n_keys // V7X_LANES)], axis=1))
        for (c, r0, n_keys, _), p, alpha in zip(units, probs, alphas):
            pv = jnp.dot(p, v_ext[:n_keys], preferred_element_type=F32)
            rows = slice(r0, r0 + q_sub)
            acc_ref[c, rows, :] = jnp.concatenate([alpha, alpha], axis=1) * acc_ref[c, rows, :] + pv

    def full_step(j, carry):
        step(j, None)
        return carry

    lax.fori_loop(0, qi * (tq // tk), full_step, 0)
    for d in range(tq // tk):
        step(qi * (tq // tk) + d, d)

    lam = (jnp.exp(jnp.sum(lq1_ref[...] * lk1_ref[...], axis=-1, keepdims=True))
           - jnp.exp(jnp.sum(lq2_ref[...] * lk2_ref[...], axis=-1, keepdims=True)) + LAM_INIT)
    a0, a1 = acc_ref[0], acc_ref[1]
    o = a0[:, :DA_V_DIM] / a0[:, DA_V_DIM:] - lam * (a1[:, :DA_V_DIM] / a1[:, DA_V_DIM:])
    o = o * lax.rsqrt(jnp.mean(o * o, axis=-1, keepdims=True) + NORM_EPS)
    o_ref[0] = (o * sw_ref[...] * (1.0 - LAM_INIT)).astype(o_ref.dtype)


def _diff_attention(q, k, v, lq1, lk1, lq2, lk2, subln_w, B, S):
    tq = _pick_tile(S, ATTN_Q_BLOCK)
    tk = _pick_tile(tq, ATTN_K_BLOCK)
    q_sub = _pick_tile(tk, ATTN_Q_SUB)
    q3, k3, v3 = (t.reshape(B, S, DA_WIDTH) for t in (q, k, v))
    vec = lambda a: a.reshape(1, -1).astype(F32)
    blk = lambda b, h, i: (b, i, h)
    seq = lambda b, h, i: (b, 0, h)
    fixed = lambda b, h, i: (0, 0)
    out = pl.pallas_call(
        functools.partial(_diff_attn_body, tq=tq, tk=tk, q_sub=q_sub),
        grid=(B, DA_HEADS, S // tq),
        in_specs=[pl.BlockSpec((1, tq, DA_V_DIM), blk),
                  pl.BlockSpec((1, S, DA_V_DIM), seq),
                  pl.BlockSpec((1, S, DA_V_DIM), seq),
                  pl.BlockSpec((1, DA_HEAD_DIM), fixed),
                  pl.BlockSpec((1, DA_HEAD_DIM), fixed),
                  pl.BlockSpec((1, DA_HEAD_DIM), fixed),
                  pl.BlockSpec((1, DA_HEAD_DIM), fixed),
                  pl.BlockSpec((1, DA_V_DIM), fixed)],
        out_specs=pl.BlockSpec((1, tq, DA_V_DIM), blk),
        out_shape=jax.ShapeDtypeStruct((B, S, DA_WIDTH), BF16),
        scratch_shapes=[pltpu.VMEM((2, tq, V7X_LANES), F32),
                        pltpu.VMEM((2, tq, 2 * DA_V_DIM), F32)],
        compiler_params=_cparams("parallel", "parallel", "arbitrary"),
        name="diff_attention",
    )(q3, k3, v3, vec(lq1), vec(lk1), vec(lq2), vec(lk2), vec(subln_w))
    return out.reshape(B * S, DA_WIDTH)


RW_CHUNK = 64
RW_BLOCK = 256
RW_PAIR = 2 * RW_HEAD
RW_INV_DOUBLINGS = RW_CHUNK.bit_length() - 2


def _bdot(a, b):
    return jnp.dot(a.astype(BF16), b.astype(BF16), preferred_element_type=F32)


def _bdot_nt(a, b):
    return lax.dot_general(a.astype(BF16), b.astype(BF16), (((1,), (1,)), ((), ())),
                           preferred_element_type=F32)


def _bdot_tn(a, b):
    return lax.dot_general(a.astype(BF16), b.astype(BF16), (((0,), (0,)), ((), ())),
                           preferred_element_type=F32)


def _split3_dot(m, x):
    mb = m.astype(BF16)
    x0 = x.astype(BF16)
    r1 = x - x0.astype(F32)
    x1 = r1.astype(BF16)
    x2 = (r1 - x1.astype(F32)).astype(BF16)
    dot = lambda t: jnp.dot(mb, t, preferred_element_type=F32)
    return dot(x0) + dot(x1) + dot(x2)


def _rwkv_body(p_ref, prev_ref, mu_ref, w0_ref, a0_ref, kk_ref, ka_ref, rk_ref, lnw_ref, lnb_ref,
               wa_ref, g2_ref, o_ref,
               s_ref, rt_ref, kh_ref, at_ref, bh_ref, v_ref, gl_ref, bonus_ref, gate_ref, y_ref, *, tb, width):
    i = pl.program_id(1)
    n_pairs = width // RW_PAIR
    C = RW_CHUNK

    @pl.when(i == 0)
    def _():
        s_ref[...] = jnp.zeros(s_ref.shape, F32)

    hr = lax.broadcasted_iota(jnp.int32, (width, width), 0) // RW_HEAD
    hc = lax.broadcasted_iota(jnp.int32, (width, width), 1) // RW_HEAD
    head_ones = jnp.where(hr == hc, 1.0, 0.0).astype(BF16)
    head_sum = lambda t: jnp.dot(t.astype(BF16), head_ones, preferred_element_type=F32)

    def stage_block():
        p = p_ref[0]
        row = lax.broadcasted_iota(jnp.int32, p.shape, 0)
        last_prev = jnp.where(i == 0, 0.0, prev_ref[0, 7:8, :])
        prev = jnp.where(row == 0, last_prev, pltpu.roll(p, 1, 0))
        ps = p + mu_ref[...] * (prev - p)
        r = ps[:, 0:width]
        k = ps[:, width:2 * width]
        v = ps[:, 2 * width:3 * width]
        c3 = 3 * width
        wa_in = ps[:, c3:c3 + DECAY_LORA + AAA_LORA]
        lane = lax.broadcasted_iota(jnp.int32, wa_in.shape, 1)
        wa_in = jnp.where(lane < DECAY_LORA, jnp.tanh(wa_in), wa_in)
        wa = _bdot(wa_in, wa_ref[...])
        gd = ps[:, c3 + DECAY_LORA + AAA_LORA:]
        g = _bdot(jax.nn.sigmoid(gd), g2_ref[...])
        wl = w0_ref[...] + wa[:, :width]
        softplus_neg = jnp.maximum(-wl, 0.0) + jnp.log1p(jnp.exp(-jnp.abs(wl)))
        logw = -jnp.exp(-softplus_neg - 0.5)
        a = jax.nn.sigmoid(a0_ref[...] + wa[:, width:])
        kk = k * kk_ref[...]
        kk = kk / jnp.maximum(jnp.sqrt(head_sum(kk * kk)), 1e-12)
        k2 = k * (1.0 + (a - 1.0) * ka_ref[...])
        bonus = head_sum(r * k2 * rk_ref[...]) * v
        tr = lax.broadcasted_iota(jnp.int32, (tb, tb), 0)
        tc = lax.broadcasted_iota(jnp.int32, (tb, tb), 1)
        chunk_tril = jnp.where((tr // C == tc // C) & (tc <= tr), 1.0, 0.0)
        cum = _split3_dot(chunk_tril, logw)
        g_inc = jnp.exp(cum)
        g_inv = jnp.exp(-cum)
        g_exc = jnp.exp(cum - logw)
        rt_ref[...] = r * g_inc
        kh_ref[...] = k2 * g_inv
        at_ref[...] = -kk * g_exc
        bh_ref[...] = kk * a * g_inv
        v_ref[...] = v
        gl_ref[...] = g_inc
        bonus_ref[...] = bonus
        gate_ref[...] = g

    stage_block()

    bonus_prev = bonus_ref[...]
    gate_prev = gate_ref[...]
    lane_p = lax.broadcasted_iota(jnp.int32, (C, RW_PAIR), 1)
    first = lane_p < RW_HEAD
    ri = lax.broadcasted_iota(jnp.int32, (RW_PAIR, RW_PAIR), 0)
    ci = lax.broadcasted_iota(jnp.int32, (RW_PAIR, RW_PAIR), 1)
    strict = ci < ri
    incl = ci <= ri
    eye = jnp.where(ci == ri, 1.0, 0.0)

    def stack(t):
        return jnp.concatenate([jnp.where(first, t, 0.0), jnp.where(first, 0.0, t)], axis=0)

    n_chunks = tb // C
    units = [(c, j) for c in range(n_chunks) for j in range(n_pairs)]
    U = range(len(units))
    rows = [slice(c * C, (c + 1) * C) for c, _ in units]
    lanes = [slice(j * RW_PAIR, (j + 1) * RW_PAIR) for _, j in units]
    rts = [stack(rt_ref[rows[n], lanes[n]]) for n in U]
    khs = [stack(kh_ref[rows[n], lanes[n]]) for n in U]
    ats = [stack(at_ref[rows[n], lanes[n]]) for n in U]
    bhs = [stack(bh_ref[rows[n], lanes[n]]) for n in U]
    vs = [stack(v_ref[rows[n], lanes[n]]) for n in U]
    g_last = [gl_ref[(c + 1) * C - 1:(c + 1) * C, lanes[n]] for n, (c, _) in enumerate(units)]

    sc = [_bdot_nt(jnp.concatenate([ats[n], rts[n]], axis=0), jnp.concatenate([bhs[n], khs[n]], axis=0))
          for n in U]
    l_ab = [jnp.where(strict, sc[n][:RW_PAIR, :RW_PAIR], 0.0) for n in U]
    l_ak = [jnp.where(strict, sc[n][:RW_PAIR, RW_PAIR:], 0.0) for n in U]
    a_rb = [jnp.where(incl, sc[n][RW_PAIR:, :RW_PAIR], 0.0) for n in U]
    a_rk = [jnp.where(incl, sc[n][RW_PAIR:, RW_PAIR:], 0.0) for n in U]

    pw = [_bdot(l_ab[n], l_ab[n]) for n in U]
    tinv = [eye + l_ab[n] for n in U]
    for _ in range(1, RW_INV_DOUBLINGS):
        both = [_bdot(jnp.concatenate([pw[n], tinv[n]], axis=0), pw[n]) for n in U]
        tinv = [tinv[n] + both[n][RW_PAIR:] for n in U]
        pw = [both[n][:RW_PAIR] for n in U]
    tinv = [tinv[n] + _bdot(tinv[n], pw[n]) for n in U]

    lv = [_bdot(l_ak[n], vs[n]) for n in U]
    g0 = [_bdot_tn(vs[n], khs[n]) for n in U]
    taw = [_bdot(tinv[n], jnp.concatenate([ats[n], lv[n]], axis=1)) for n in U]
    ar = [_bdot(jnp.concatenate([a_rb[n], a_rk[n]], axis=1),
                jnp.concatenate([taw[n], jnp.concatenate([jnp.zeros_like(vs[n]), vs[n]], axis=1)], axis=0))
          for n in U]
    qt_ta = [jnp.concatenate([rts[n] + ar[n][:, :RW_PAIR], taw[n][:, :RW_PAIR]], axis=0) for n in U]

    state = [s_ref[j] for j in range(n_pairs)]
    for c in range(n_chunks):
        ns = [c * n_pairs + j for j in range(n_pairs)]
        m1 = [_bdot_nt(qt_ta[n], state[j]) for j, n in enumerate(ns)]
        u = [m1[j][RW_PAIR:] + taw[n][:, RW_PAIR:] for j, n in enumerate(ns)]
        state = [(state[j] + g0[n] + _bdot_tn(u[j], bhs[n])) * g_last[n] for j, n in enumerate(ns)]
        for j, n in enumerate(ns):
            y = m1[j][:RW_PAIR] + ar[n][:, RW_PAIR:]
            y_ref[rows[n], lanes[n]] = y[:C] + y[C:]
    for j in range(n_pairs):
        s_ref[j] = state[j]

    y = y_ref[...]
    inv_n = 1.0 / RW_HEAD
    mean = head_sum(y) * inv_n
    yc = y - mean
    var = head_sum(yc * yc) * inv_n
    yn = yc * lax.rsqrt(var + GN_EPS) * lnw_ref[...] + lnb_ref[...]
    o_ref[0] = ((yn + bonus_prev) * gate_prev).astype(o_ref.dtype)


def _rwkv7(p_rw, shift_mu, w0, w2, a0, a2, g2, k_k, k_a, r_k, lnx_w, lnx_b, B, S):
    cols = p_rw.shape[-1]
    width = (cols - DECAY_LORA - AAA_LORA - GATE_LORA) // 3
    tb = _pick_tile(S, RW_BLOCK)
    assert tb % RW_CHUNK == 0 and width % RW_PAIR == 0
    p3 = p_rw.reshape(B, S, cols)
    vec = lambda t: t.reshape(1, -1).astype(F32)
    zeros = jnp.zeros_like(w2)
    wa = jnp.concatenate([jnp.concatenate([w2, zeros], axis=1),
                          jnp.concatenate([jnp.zeros_like(a2), a2], axis=1)], axis=0).astype(BF16)
    blk = lambda b, i: (b, i, 0)
    prev_blk = lambda b, i: (b, jnp.maximum(i * (tb // 8) - 1, 0), 0)
    fixed = lambda b, i: (0, 0)
    wide = pltpu.VMEM((tb, width), F32)
    out = pl.pallas_call(
        functools.partial(_rwkv_body, tb=tb, width=width),
        grid=(B, S // tb),
        in_specs=[pl.BlockSpec((1, tb, cols), blk),
                  pl.BlockSpec((1, 8, cols), prev_blk),
                  pl.BlockSpec((1, cols), fixed)]
                 + [pl.BlockSpec((1, width), fixed)] * 7
                 + [pl.BlockSpec((DECAY_LORA + AAA_LORA, 2 * width), fixed),
                    pl.BlockSpec((GATE_LORA, width), fixed)],
        out_specs=pl.BlockSpec((1, tb, width), blk),
        out_shape=jax.ShapeDtypeStruct((B, S, width), BF16),
        scratch_shapes=[pltpu.VMEM((width // RW_PAIR, RW_PAIR, RW_PAIR), F32)] + [wide] * 9,
        compiler_params=_cparams("parallel", "arbitrary"),
        name="rwkv7_mix",
    )(p3, p3, vec(shift_mu), vec(w0), vec(a0), vec(k_k), vec(k_a), vec(r_k), vec(lnx_w), vec(lnx_b),
      wa, g2.astype(BF16))
    return out.reshape(B * S, width)


def _outproj_router_body(oda_ref, orw_ref, x_ref, wa_ref, wb_ref, nw_ref, rw_ref, rb_ref,
                         x1_ref, h_ref, idx_ref, gate_ref, rank_ref, cnt_ref, carry_ref):
    i = pl.program_id(0)

    @pl.when(i == 0)
    def _():
        carry_ref[...] = jnp.zeros(carry_ref.shape, F32)

    x1 = (x_ref[...] + jnp.dot(oda_ref[...], wa_ref[...], preferred_element_type=F32)
          + jnp.dot(orw_ref[...], wb_ref[...], preferred_element_type=F32))
    x1_ref[...] = x1
    h = (x1 * lax.rsqrt(jnp.mean(x1 * x1, axis=-1, keepdims=True) + NORM_EPS)) * nw_ref[...]
    h_ref[...] = h
    h_hi = h.astype(BF16)
    h_lo = (h - h_hi.astype(F32)).astype(BF16)
    rw = rw_ref[...]
    rw_hi = rw.astype(BF16)
    rw_lo = (rw - rw_hi.astype(F32)).astype(BF16)
    n_exp = rw.shape[1]
    terms = jnp.dot(jnp.concatenate([h_hi, h_lo], axis=0), jnp.concatenate([rw_hi, rw_lo], axis=1),
                    preferred_element_type=F32)
    n_tok = h.shape[0]
    logits = terms[:n_tok, :n_exp] + terms[n_tok:, :n_exp] + terms[:n_tok, n_exp:] + rb_ref[...]
    tm, n_exp = logits.shape
    lane = lax.broadcasted_iota(jnp.int32, logits.shape, 1).astype(F32)
    out_lane = lax.broadcasted_iota(jnp.int32, (tm, TOP_K), 1)

    vals, idxs = [], []
    rest = logits
    for _ in range(TOP_K):
        m = jnp.max(rest, axis=-1, keepdims=True)
        sel = jnp.min(jnp.where(rest == m, lane, float(n_exp)), axis=-1, keepdims=True)
        vals.append(m)
        idxs.append(sel)
        rest = jnp.where(lane == sel, -jnp.inf, rest)
    exps = [jnp.exp(v - vals[0]) for v in vals]
    denom = exps[0]
    for e in exps[1:]:
        denom = denom + e

    hot = [jnp.where(lane == s, 1.0, 0.0) for s in idxs]
    hot_all = hot[0]
    for t in hot[1:]:
        hot_all = hot_all + t
    tr = lax.broadcasted_iota(jnp.int32, (tm, tm), 0)
    tc = lax.broadcasted_iota(jnp.int32, (tm, tm), 1)
    before = jnp.dot(jnp.where(tc < tr, 1.0, 0.0).astype(BF16), hot_all.astype(BF16),
                     preferred_element_type=F32) + carry_ref[...]

    idx_out = jnp.zeros((tm, TOP_K), F32)
    gate_out = jnp.zeros((tm, TOP_K), F32)
    rank_out = jnp.zeros((tm, TOP_K), F32)
    for kk in range(TOP_K):
        rank_k = jnp.sum(hot[kk] * before, axis=-1, keepdims=True)
        idx_out = jnp.where(out_lane == kk, idxs[kk], idx_out)
        gate_out = jnp.where(out_lane == kk, exps[kk] / denom, gate_out)
        rank_out = jnp.where(out_lane == kk, rank_k, rank_out)
    idx_ref[...] = idx_out.astype(jnp.int32)
    gate_ref[...] = gate_out
    rank_ref[...] = rank_out.astype(jnp.int32)
    carry_ref[...] += jnp.sum(hot_all, axis=0, keepdims=True)
    cnt_ref[...] = carry_ref[...]


def _outproj_router(o_da, o_rw, x2, w_out, ffn_norm_w, router_w, router_b):
    T, D = x2.shape
    n_exp = router_w.shape[1]
    tm = _pick_tile(T, PROJ_ROWS)
    wa = w_out[:DA_WIDTH].astype(BF16)
    wb = w_out[DA_WIDTH:].astype(BF16)
    row = lambda i: (i, 0)
    fixed = lambda i: (0, 0)
    return pl.pallas_call(
        _outproj_router_body,
        grid=(T // tm,),
        in_specs=[pl.BlockSpec((tm, DA_WIDTH), row),
                  pl.BlockSpec((tm, o_rw.shape[1]), row),
                  pl.BlockSpec((tm, D), row),
                  pl.BlockSpec(wa.shape, fixed),
                  pl.BlockSpec(wb.shape, fixed),
                  pl.BlockSpec((1, D), fixed),
                  pl.BlockSpec((D, n_exp), fixed),
                  pl.BlockSpec((1, n_exp), fixed)],
        out_specs=[pl.BlockSpec((tm, D), row),
                   pl.BlockSpec((tm, D), row),
                   pl.BlockSpec((tm, TOP_K), row),
                   pl.BlockSpec((tm, TOP_K), row),
                   pl.BlockSpec((tm, TOP_K), row),
                   pl.BlockSpec((1, n_exp), fixed)],
        out_shape=[jax.ShapeDtypeStruct((T, D), F32),
                   jax.ShapeDtypeStruct((T, D), F32),
                   jax.ShapeDtypeStruct((T, TOP_K), jnp.int32),
                   jax.ShapeDtypeStruct((T, TOP_K), F32),
                   jax.ShapeDtypeStruct((T, TOP_K), jnp.int32),
                   jax.ShapeDtypeStruct((1, n_exp), F32)],
        scratch_shapes=[pltpu.VMEM((1, n_exp), F32)],
        compiler_params=_cparams("arbitrary"),
        name="outproj_router",
    )(o_da, o_rw, x2, wa, wb, ffn_norm_w.reshape(1, D), router_w, router_b.reshape(1, n_exp))


MOE_ROWS = 512
DISPATCH_TOKENS = 256
ISSUE_UNROLL = 8
MXU_TILE = 256
SUBLANES = 8


def _store_row_tiled(ref, x, lead=(), first_row=0):
    n = x.shape[0]
    for c in range(x.shape[1] // V7X_LANES):
        rows = pl.ds(first_row * SUBLANES + c, n, stride=SUBLANES)
        ref[lead + (rows, slice(None))] = x[:, c * V7X_LANES:(c + 1) * V7X_LANES]


def _load_row_tiled(ref, n, c, lead=(), first_row=0):
    return ref[lead + (pl.ds(first_row * SUBLANES + c, n, stride=SUBLANES), slice(None))]


def _tile_rows(r):
    return pl.multiple_of(r * SUBLANES, SUBLANES)


def _regroup_expert_weights(w1_ref, w2_ref, o1_ref, o2_ref):
    half = MXU_TILE // 2
    r = lax.broadcasted_iota(jnp.int32, (MXU_TILE, MXU_TILE), 0)
    c = lax.broadcasted_iota(jnp.int32, (MXU_TILE, MXU_TILE), 1)
    perm = jnp.where(c == (r >> 1) + (r & 1) * half, 1.0, 0.0).astype(BF16)
    for g in range(w1_ref.shape[2] // MXU_TILE):
        sl = slice(g * MXU_TILE, (g + 1) * MXU_TILE)
        o1_ref[0, :, sl] = jnp.dot(w1_ref[0, :, sl].astype(BF16), perm,
                                   preferred_element_type=F32).astype(o1_ref.dtype)
    o2_ref[...] = w2_ref[...].astype(o2_ref.dtype)


def _dispatch_body(pad_ref, dest_ref, h_ref, w1_ref, w2_ref, xs_hbm, o1_ref, o2_ref,
                   stage_ref, zero_ref, sems, zero_sem, *, steps_per_expert):
    i = pl.program_id(0)
    n_tok = h_ref.shape[0]

    @pl.when(i == 0)
    def _():
        zero_ref[...] = jnp.zeros(zero_ref.shape, zero_ref.dtype)

        def fill(start_row):
            cp = pltpu.make_async_copy(zero_ref, xs_hbm.at[pl.ds(_tile_rows(start_row), zero_ref.shape[0])],
                                       zero_sem)
            cp.start()
            cp.wait()

        def fill_pad(e, carry):
            fill(pad_ref[e])
            return carry

        def fill_tail(b, carry):
            fill(b * MOE_ROWS)
            return carry

        n_exp = pad_ref.shape[0] - 1
        lax.fori_loop(0, n_exp, fill_pad, 0)
        lax.fori_loop(pad_ref[n_exp], xs_hbm.shape[0] // zero_ref.shape[0], fill_tail, 0)

    slot = i % 2
    _store_row_tiled(stage_ref, h_ref[...], lead=(slot,))

    def issue(t, carry):
        src = stage_ref.at[slot, pl.ds(_tile_rows(t), SUBLANES)]
        for kk in range(TOP_K):
            dst = xs_hbm.at[pl.ds(_tile_rows(dest_ref[0, 0, t * TOP_K + kk]), SUBLANES)]
            pltpu.make_async_copy(src, dst, sems.at[slot]).start()
        return carry

    lax.fori_loop(0, n_tok, issue, 0, unroll=ISSUE_UNROLL)

    @pl.when(i % steps_per_expert == 0)
    def _():
        _regroup_expert_weights(w1_ref, w2_ref, o1_ref, o2_ref)

    def drain(s):
        for _ in range(TOP_K):
            pltpu.make_async_copy(stage_ref.at[s], xs_hbm.at[pl.ds(0, stage_ref.shape[1])], sems.at[s]).wait()

    @pl.when(i > 0)
    def _():
        drain(1 - slot)

    @pl.when(i == pl.num_programs(0) - 1)
    def _():
        drain(slot)


def _dispatch(h, dest, pad_start, n_padded, mlp1_w, mlp2_w):
    T, D = h.shape
    assert D == SUBLANES * V7X_LANES, "one logical row must be exactly one (8, 128) tile"
    n_exp = mlp1_w.shape[0]
    steps_per_expert = max(1, T // (n_exp * DISPATCH_TOKENS))
    n_steps = n_exp * steps_per_expert
    n_tok = T // n_steps
    assert n_tok * n_steps == T and n_tok % ISSUE_UNROLL == 0
    dest3 = dest.reshape(n_steps, 1, n_tok * TOP_K)
    wsel = lambda i, pad: (i // steps_per_expert, 0, 0)
    spec1 = pl.BlockSpec((1,) + mlp1_w.shape[1:], wsel)
    spec2 = pl.BlockSpec((1,) + mlp2_w.shape[1:], wsel)
    return pl.pallas_call(
        functools.partial(_dispatch_body, steps_per_expert=steps_per_expert),
        grid_spec=pltpu.PrefetchScalarGridSpec(
            num_scalar_prefetch=1,
            grid=(n_steps,),
            in_specs=[pl.BlockSpec((1, 1, n_tok * TOP_K), lambda i, pad: (i, 0, 0), memory_space=pltpu.SMEM),
                      pl.BlockSpec((n_tok, D), lambda i, pad: (i, 0)),
                      spec1, spec2],
            out_specs=[pl.BlockSpec(memory_space=pl.ANY), spec1, spec2],
            scratch_shapes=[pltpu.VMEM((2, n_tok * SUBLANES, V7X_LANES), h.dtype),
                            pltpu.VMEM((MOE_ROWS * SUBLANES, V7X_LANES), h.dtype),
                            pltpu.SemaphoreType.DMA((2,)),
                            pltpu.SemaphoreType.DMA(())]),
        out_shape=[jax.ShapeDtypeStruct((n_padded * SUBLANES, V7X_LANES), h.dtype),
                   jax.ShapeDtypeStruct(mlp1_w.shape, BF16),
                   jax.ShapeDtypeStruct(mlp2_w.shape, BF16)],
        compiler_params=_cparams("arbitrary"),
        name="moe_dispatch",
    )(pad_start, dest3, h, mlp1_w, mlp2_w)


def _experts_body(be_ref, nu_ref, xs_ref, w1_ref, b1_ref, w2_ref, b2_ref, y_ref):
    del be_ref
    i = pl.program_id(0)
    half = MXU_TILE // 2

    @pl.when(i < nu_ref[0])
    def _():
        n_rows = xs_ref.shape[0] // SUBLANES
        x = jnp.concatenate([_load_row_tiled(xs_ref, n_rows, c).astype(BF16) for c in range(SUBLANES)], axis=1)
        gu = jnp.dot(x, w1_ref[0], preferred_element_type=F32) + b1_ref[0]
        acts = []
        for g in range(gu.shape[1] // MXU_TILE):
            gate = jnp.minimum(gu[:, g * MXU_TILE:g * MXU_TILE + half], SWIGLU_LIMIT)
            up = jnp.clip(gu[:, g * MXU_TILE + half:(g + 1) * MXU_TILE], -SWIGLU_LIMIT, SWIGLU_LIMIT)
            acts.append(((up + 1.0) * gate * jax.nn.sigmoid(SWIGLU_ALPHA * gate)).astype(BF16))
        act = jnp.concatenate(acts, axis=1)
        _store_row_tiled(y_ref, jnp.dot(act, w2_ref[0], preferred_element_type=F32) + b2_ref[0])

    @pl.when(i >= nu_ref[0])
    def _():
        y_ref[...] = jnp.zeros(y_ref.shape, y_ref.dtype)


def _experts(xs, blk_expert, n_used, w1, b1, w2, b2):
    blk_rows = MOE_ROWS * SUBLANES
    n_blocks = xs.shape[0] // blk_rows
    D, gu_cols = w1.shape[1], w1.shape[2]
    xrow = lambda i, be, nu: (jnp.maximum(jnp.minimum(i, nu[0] - 1), 0), 0)
    yrow = lambda i, be, nu: (i, 0)
    wsel = lambda i, be, nu: (be[i], 0, 0)
    return pl.pallas_call(
        _experts_body,
        grid_spec=pltpu.PrefetchScalarGridSpec(
            num_scalar_prefetch=2,
            grid=(n_blocks,),
            in_specs=[pl.BlockSpec((blk_rows, V7X_LANES), xrow),
                      pl.BlockSpec((1, D, gu_cols), wsel),
                      pl.BlockSpec((1, 1, gu_cols), wsel),
                      pl.BlockSpec((1, gu_cols // 2, D), wsel),
                      pl.BlockSpec((1, 1, D), wsel)],
            out_specs=pl.BlockSpec((blk_rows, V7X_LANES), yrow)),
        out_shape=jax.ShapeDtypeStruct(xs.shape, F32),
        compiler_params=_cparams("arbitrary"),
        name="moe_experts",
    )(blk_expert, n_used, xs, w1, b1, w2, b2)


def _combine_body(dest_ref, next_ref, ys_hbm, x1_ref, gate_ref, fw_ref, o_ref, buf_ref, sems):
    i = pl.program_id(0)
    slot = i % 2

    def gather(d_ref, s):
        def issue(t, carry):
            for kk in range(TOP_K):
                src = ys_hbm.at[pl.ds(_tile_rows(d_ref[0, 0, t * TOP_K + kk]), SUBLANES)]
                pltpu.make_async_copy(src, buf_ref.at[s, kk, pl.ds(_tile_rows(t), SUBLANES)], sems.at[s]).start()
            return carry

        lax.fori_loop(0, DISPATCH_TOKENS, issue, 0, unroll=ISSUE_UNROLL)

    @pl.when(i == 0)
    def _():
        gather(dest_ref, 0)

    @pl.when(i + 1 < pl.num_programs(0))
    def _():
        gather(next_ref, 1 - slot)

    for kk in range(TOP_K):
        pltpu.make_async_copy(ys_hbm.at[pl.ds(0, buf_ref.shape[2])], buf_ref.at[slot, kk], sems.at[slot]).wait()

    gates = gate_ref[...]
    pieces = []
    sq = None
    for c in range(SUBLANES):
        piece = x1_ref[:, c * V7X_LANES:(c + 1) * V7X_LANES]
        for kk in range(TOP_K):
            piece = piece + gates[:, kk:kk + 1] * _load_row_tiled(buf_ref, DISPATCH_TOKENS, c, lead=(slot, kk))
        pieces.append(piece)
        sq = piece * piece if sq is None else sq + piece * piece
    inv = lax.rsqrt(jnp.sum(sq, axis=-1, keepdims=True) * (1.0 / (SUBLANES * V7X_LANES)) + NORM_EPS)
    for c, piece in enumerate(pieces):
        lanes = slice(c * V7X_LANES, (c + 1) * V7X_LANES)
        o_ref[:, lanes] = piece * inv * fw_ref[:, lanes]


def _combine(ys, dest, x1, gates, final_w):
    T, D = x1.shape
    n_blocks = T // DISPATCH_TOKENS
    dest3 = dest.reshape(n_blocks, 1, DISPATCH_TOKENS * TOP_K)
    row = lambda i: (i, 0)
    idx_blk = (1, 1, DISPATCH_TOKENS * TOP_K)
    return pl.pallas_call(
        _combine_body,
        grid=(n_blocks,),
        in_specs=[pl.BlockSpec(idx_blk, lambda i: (i, 0, 0), memory_space=pltpu.SMEM),
                  pl.BlockSpec(idx_blk, lambda i: (jnp.minimum(i + 1, n_blocks - 1), 0, 0),
                               memory_space=pltpu.SMEM),
                  pl.BlockSpec(memory_space=pl.ANY),
                  pl.BlockSpec((DISPATCH_TOKENS, D), row),
                  pl.BlockSpec((DISPATCH_TOKENS, TOP_K), row),
                  pl.BlockSpec((1, D), lambda i: (0, 0))],
        out_specs=pl.BlockSpec((DISPATCH_TOKENS, D), row),
        out_shape=jax.ShapeDtypeStruct((T, D), F32),
        scratch_shapes=[pltpu.VMEM((2, TOP_K, DISPATCH_TOKENS * SUBLANES, V7X_LANES), F32),
                        pltpu.SemaphoreType.DMA((2,))],
        compiler_params=_cparams("arbitrary"),
        name="moe_combine",
    )(dest3, dest3, ys, x1, gates, final_w.reshape(1, D))


def _moe(h, x1, idx, gates, rank, counts, mlp1_w, mlp1_b, mlp2_w, mlp2_b, final_w):
    T, D = h.shape
    n_exp = mlp1_w.shape[0]
    assert T % DISPATCH_TOKENS == 0
    n_padded = T * TOP_K + n_exp * MOE_ROWS
    n_blocks = n_padded // MOE_ROWS
    counts = counts.reshape(n_exp).astype(jnp.int32)
    padded = (counts + MOE_ROWS - 1) // MOE_ROWS * MOE_ROWS
    pend = jnp.cumsum(padded)
    pstart = pend - padded
    dest = (pstart[idx] + rank).astype(jnp.int32)
    starts = jnp.arange(n_blocks, dtype=jnp.int32) * MOE_ROWS
    blk_expert = jnp.minimum(jnp.sum((pend[None, :] <= starts[:, None]).astype(jnp.int32), axis=1),
                             n_exp - 1).astype(jnp.int32)
    n_used = (pend[-1:] // MOE_ROWS).astype(jnp.int32)

    half = MXU_TILE // 2
    b1 = mlp1_b.reshape(n_exp, -1, half, 2).transpose(0, 1, 3, 2).reshape(n_exp, 1, -1)
    pad_start = jnp.minimum(pstart + counts, n_padded - MOE_ROWS).astype(jnp.int32)
    xs, w1, w2 = _dispatch(h, dest, jnp.concatenate([pad_start, n_used]), n_padded, mlp1_w, mlp2_w)
    ys = _experts(xs, blk_expert, n_used, w1, b1, w2, mlp2_b[:, None, :])
    return _combine(ys, dest, x1, gates, final_w)


def kernel(x, positions, attn_norm_w, w_in, shift_mu, lambda_q1, lambda_k1, lambda_q2, lambda_k2, subln_w,
           rw_w0, rw_w2, rw_a0, rw_a2, rw_g2, rw_k_k, rw_k_a, rw_r_k, rw_lnx_w, rw_lnx_b, w_out, ffn_norm_w,
           router_w, router_b, mlp1_w, mlp1_b, mlp2_w, mlp2_b, final_norm_w):
    B, S, D = x.shape
    assert attn_norm_w.shape[0] == 1, "single-layer stack"
    l = LAYER_INDEX
    x2 = x.reshape(B * S, D)
    cos_t, sin_t = _rope_tables(positions)
    q, k, v, p_rw = _input_projection(x2, attn_norm_w[l], w_in[l], cos_t, sin_t)
    o_da = _diff_attention(q, k, v, lambda_q1[l], lambda_k1[l], lambda_q2[l], lambda_k2[l], subln_w[l], B, S)
    o_rw = _rwkv7(p_rw, shift_mu[l], rw_w0[l], rw_w2[l], rw_a0[l], rw_a2[l], rw_g2[l],
                  rw_k_k[l], rw_k_a[l], rw_r_k[l], rw_lnx_w[l], rw_lnx_b[l], B, S)
    x1, h, idx, gates, rank, counts = _outproj_router(o_da, o_rw, x2, w_out[l], ffn_norm_w[l],
                                                      router_w[l], router_b[l])
    out = _moe(h, x1, idx, gates, rank, counts, mlp1_w[l], mlp1_b[l], mlp2_w[l], mlp2_b[l], final_norm_w)
    return out.reshape(B, S, D)
```
